```python
import jax, jax.numpy as jnp
from jax import lax
import numpy as np

D_MODEL = 1024
BATCH = 16
SEQ = 4096
DEPTH = 2

GRID_W = 64
N_META = 16
N_MIXERS = 2
NA_HEADS = 16
NA_HEAD_DIM = D_MODEL // NA_HEADS
NA_WIN_ROWS = 8
NA_WIN_COLS = 16
GQA_HEADS = 16
GQA_KV_HEADS = 4
GQA_GROUP = GQA_HEADS // GQA_KV_HEADS
GQA_HEAD_DIM = D_MODEL // GQA_HEADS
Q_BLOCK = 128
ROPE_THETA = 10000.0
D_FF = 2816
RMS_EPS = 1e-6
N_NA_LAYERS = (DEPTH + 1) // 2
N_GQA_LAYERS = DEPTH // 2

kernel_name = "hybrid_natten_gqa_macaron_encoder"


def rms_norm(x, gain):
    xf = x.astype(jnp.float32)
    y = xf * lax.rsqrt(jnp.mean(xf * xf, axis=-1, keepdims=True) + RMS_EPS)
    return (y * gain.astype(jnp.float32)).astype(x.dtype)


def swiglu(x, w_gate, w_up, w_down):
    return (jax.nn.silu(x @ w_gate) * (x @ w_up)) @ w_down


def neighborhood_attention(h, w_qkv, w_o, rpb, meta_bias):
    b, l, d = h.shape
    n = l - N_META
    rows = n // GRID_W
    kh = min(NA_WIN_ROWS, rows)
    kw = NA_WIN_COLS
    qkv = (h @ w_qkv).reshape(b, l, 3, NA_HEADS, NA_HEAD_DIM)
    q = qkv[:, :, 0] * (NA_HEAD_DIM ** -0.5)
    k = qkv[:, :, 1]
    v = qkv[:, :, 2]
    q_m, k_m, v_m = q[:, :N_META], k[:, :N_META], v[:, :N_META]
    q_g = q[:, N_META:].reshape(b, rows, GRID_W, NA_HEADS, NA_HEAD_DIM)
    k_g = k[:, N_META:].reshape(b, rows, GRID_W, NA_HEADS, NA_HEAD_DIM)
    v_g = v[:, N_META:].reshape(b, rows, GRID_W, NA_HEADS, NA_HEAD_DIM)

    s_mm = jnp.einsum('bqhd,bkhd->bhqk', q_m, k_m).astype(jnp.float32)
    p_mm = jax.nn.softmax(s_mm, axis=-1).astype(v.dtype)
    o_meta = jnp.einsum('bhqk,bkhd->bqhd', p_mm, v_m).reshape(b, N_META, d)

    cols = np.arange(GRID_W)
    col_start = np.clip(cols - kw // 2, 0, GRID_W - kw)
    col_idx = col_start[:, None] + np.arange(kw)[None, :]
    col_rel = col_idx - cols[:, None] + (NA_WIN_COLS - 1)
    rpb_cols = rpb[:, :, col_rel].astype(jnp.float32)
    mbias = meta_bias.astype(jnp.float32)[None, :, None, :]

    def row_block(r):
        rs = jnp.clip(r - kh // 2, 0, rows - kh)
        q_r = lax.dynamic_index_in_dim(q_g, r, axis=1, keepdims=False)
        k_band = lax.dynamic_slice_in_dim(k_g, rs, kh, axis=1)
        v_band = lax.dynamic_slice_in_dim(v_g, rs, kh, axis=1)
        k_win = k_band[:, :, col_idx]
        v_win = v_band[:, :, col_idx]
        row_rel = rs + jnp.arange(kh) - r + (NA_WIN_ROWS - 1)
        bias = jnp.take(rpb_cols, row_rel, axis=1).transpose(0, 2, 1, 3)
        s_win = jnp.einsum('bchd,brcjhd->bhcrj', q_r, k_win).astype(jnp.float32) + bias[None]
        s_win = s_win.reshape(b, NA_HEADS, GRID_W, kh * kw)
        s_meta = jnp.einsum('bchd,bmhd->bhcm', q_r, k_m).astype(jnp.float32) + mbias
        p = jax.nn.softmax(jnp.concatenate([s_meta, s_win], axis=-1), axis=-1).astype(v.dtype)
        p_meta = p[..., :N_META]
        p_win = p[..., N_META:].reshape(b, NA_HEADS, GRID_W, kh, kw)
        return (jnp.einsum('bhcm,bmhd->bchd', p_meta, v_m)
                + jnp.einsum('bhcrj,brcjhd->bchd', p_win, v_win))

    o_grid = lax.map(row_block, jnp.arange(rows))
    o_grid = jnp.moveaxis(o_grid, 0, 1).reshape(b, n, d)
    o = jnp.concatenate([o_meta, o_grid], axis=1)
    return o @ w_o


def axial_rope_tables(n):
    t = jnp.arange(n)
    row = (t // GRID_W).astype(jnp.float32)
    col = (t % GRID_W).astype(jnp.float32)
    sec = GQA_HEAD_DIM // 2
    freqs = ROPE_THETA ** (-jnp.arange(0, sec, 2, dtype=jnp.float32) / sec)
    ang = jnp.concatenate([row[:, None] * freqs, col[:, None] * freqs], axis=-1)
    ang = jnp.concatenate([jnp.zeros((N_META, sec), jnp.float32), ang], axis=0)
    ang = ang.reshape(N_META + n, 2, sec // 2)
    return jnp.cos(ang), jnp.sin(ang)


def apply_axial_rope(x, cos, sin):
    quarter = GQA_HEAD_DIM // 4
    xs = x.astype(jnp.float32).reshape(*x.shape[:-1], 2, 2, quarter)
    x1, x2 = xs[..., 0, :], xs[..., 1, :]
    c = cos[None, :, None]
    s = sin[None, :, None]
    out = jnp.stack([x1 * c - x2 * s, x2 * c + x1 * s], axis=-2)
    return out.reshape(x.shape).astype(x.dtype)


def grouped_query_attention(h, w_qkv, w_o, q_gain, k_gain):
    b, l, d = h.shape
    n = l - N_META
    nq = GQA_HEADS * GQA_HEAD_DIM
    nkv = GQA_KV_HEADS * GQA_HEAD_DIM
    qkv = h @ w_qkv
    q = qkv[..., :nq].reshape(b, l, GQA_HEADS, GQA_HEAD_DIM)
    k = qkv[..., nq:nq + nkv].reshape(b, l, GQA_KV_HEADS, GQA_HEAD_DIM)
    v = qkv[..., nq + nkv:].reshape(b, l, GQA_KV_HEADS, GQA_HEAD_DIM)
    q = rms_norm(q, q_gain)
    k = rms_norm(k, k_gain)
    cos, sin = axial_rope_tables(n)
    q = apply_axial_rope(q, cos, sin) * (GQA_HEAD_DIM ** -0.5)
    k = apply_axial_rope(k, cos, sin)
    q = q.reshape(b, l, GQA_KV_HEADS, GQA_GROUP, GQA_HEAD_DIM)

    def attend(q_blk):
        s = jnp.einsum('bqkgd,bskd->bkgqs', q_blk, k).astype(jnp.float32)
        p = jax.nn.softmax(s, axis=-1).astype(v.dtype)
        return jnp.einsum('bkgqs,bskd->bqkgd', p, v)

    o_meta = attend(q[:, :N_META]).reshape(b, N_META, nq)
    q_blocks = q[:, N_META:].reshape(b, n // Q_BLOCK, Q_BLOCK, GQA_KV_HEADS, GQA_GROUP, GQA_HEAD_DIM)
    o_blocks = lax.map(attend, jnp.swapaxes(q_blocks, 0, 1))
    o_real = jnp.swapaxes(o_blocks, 0, 1).reshape(b, n, nq)
    o = jnp.concatenate([o_meta, o_real], axis=1)
    return o @ w_o


def setup_inputs(seed: int = 0) -> dict:
    key = jax.random.key(seed)
    ks = jax.random.split(key, 16)
    f32 = jnp.float32
    d, f = D_MODEL, D_FF
    nq = GQA_HEADS * GQA_HEAD_DIM
    nkv = GQA_KV_HEADS * GQA_HEAD_DIM
    return {
        "x": jax.random.normal(ks[0], (BATCH, SEQ, d), f32),
        "meta_tokens": jax.random.normal(ks[1], (N_META, d), f32),
        "norm_gains": 1.0 + 0.05 * jax.random.normal(ks[2], (DEPTH, 6, d), f32),
        "ffn_w_gate": jax.random.normal(ks[3], (DEPTH, 2, d, f), f32) * d ** -0.5,
        "ffn_w_up": jax.random.normal(ks[4], (DEPTH, 2, d, f), f32) * d ** -0.5,
        "ffn_w_down": jax.random.normal(ks[5], (DEPTH, 2, f, d), f32) * f ** -0.5,
        "na_w_qkv": jax.random.normal(ks[6], (N_NA_LAYERS, d, 3 * d), f32) * d ** -0.5,
        "na_w_o": jax.random.normal(ks[7], (N_NA_LAYERS, d, d), f32) * d ** -0.5,
        "na_rpb": 0.1 * jax.random.normal(ks[8], (N_NA_LAYERS, NA_HEADS, 2 * NA_WIN_ROWS - 1, 2 * NA_WIN_COLS - 1), f32),
        "na_meta_bias": 0.1 * jax.random.normal(ks[9], (N_NA_LAYERS, NA_HEADS, N_META), f32),
        "gqa_w_qkv": jax.random.normal(ks[10], (N_GQA_LAYERS, d, nq + 2 * nkv), f32) * d ** -0.5,
        "gqa_w_o": jax.random.normal(ks[11], (N_GQA_LAYERS, nq, d), f32) * nq ** -0.5,
        "gqa_q_gain": 1.0 + 0.05 * jax.random.normal(ks[12], (N_GQA_LAYERS, GQA_HEAD_DIM), f32),
        "gqa_k_gain": 1.0 + 0.05 * jax.random.normal(ks[13], (N_GQA_LAYERS, GQA_HEAD_DIM), f32),
    }


def reference(x, meta_tokens, norm_gains, ffn_w_gate, ffn_w_up, ffn_w_down,
              na_w_qkv, na_w_o, na_rpb, na_meta_bias,
              gqa_w_qkv, gqa_w_o, gqa_q_gain, gqa_k_gain):
    b = x.shape[0]
    meta = jnp.broadcast_to(meta_tokens[None].astype(x.dtype), (b, N_META, D_MODEL))
    h = jnp.concatenate([meta, x], axis=1)
    for i in range(DEPTH):
        g = norm_gains[i]
        ff = swiglu(rms_norm(h, g[0]), ffn_w_gate[i, 0], ffn_w_up[i, 0], ffn_w_down[i, 0])
        h = h + 0.5 * rms_norm(ff, g[1])
        m_in = rms_norm(h, g[2])
        j = i // N_MIXERS
        if i % N_MIXERS == 0:
            m = neighborhood_attention(m_in, na_w_qkv[j], na_w_o[j], na_rpb[j], na_meta_bias[j])
        else:
            m = grouped_query_attention(m_in, gqa_w_qkv[j], gqa_w_o[j], gqa_q_gain[j], gqa_k_gain[j])
        h = h + rms_norm(m, g[3])
        ff = swiglu(rms_norm(h, g[4]), ffn_w_gate[i, 1], ffn_w_up[i, 1], ffn_w_down[i, 1])
        h = h + 0.5 * rms_norm(ff, g[5])
    return h[:, N_META:]
```

```python
import functools

import numpy as np
import jax
import jax.numpy as jnp
from jax import lax
from jax.experimental import pallas as pl
from jax.experimental.pallas import tpu as pltpu

D = 1024
B = 16
SEQ = 4096
DEPTH = 2
GRID_W = 64
ROWS = SEQ // GRID_W
N_META = 16
HEADS = 16
DH = 64
WIN_R = 8
WIN_C = 16
KV_HEADS = 4
D_FF = 2816
EPS = 1e-6
ROPE_THETA = 10000.0

LANES = 128
N_SLABS = D // LANES
M_X = B * SEQ
M_META = B * N_META
TM = 1024
N_XT = M_X // TM
M_PAD = M_X + TM
N_T = M_PAD // TM
TF = 256
NF = D_FF // TF
BAND = WIN_R * GRID_W
TQ = 512
TK = 512
NKC = SEQ // TK
NEG = -1e30

F32 = jnp.float32
BF16 = jnp.bfloat16


def _dot(a, b):
    return jnp.dot(a, b, preferred_element_type=F32)


def _dot_nt(a, b):
    return lax.dot_general(a, b, (((1,), (1,)), ((), ())), preferred_element_type=F32)


def _params(sem, vmem_mib):
    return pltpu.CompilerParams(dimension_semantics=sem, vmem_limit_bytes=vmem_mib * 2 ** 20)


def _rms(x, gain):
    ms = jnp.mean(x * x, axis=-1, keepdims=True)
    return x * lax.rsqrt(ms + EPS) * gain


def _ffn_body(h_ref, gpre_ref, gpost_ref, wg_ref, wu_ref, wd_ref, o_ref, xn_ref, acc_ref):
    f = pl.program_id(1)

    @pl.when(f == 0)
    def _():
        xn_ref[...] = _rms(h_ref[...], gpre_ref[...]).astype(BF16)

    xn = xn_ref[...]
    g = _dot(xn, wg_ref[...])
    u = _dot(xn, wu_ref[...])
    a = (g * jax.nn.sigmoid(g) * u).astype(BF16)
    part = _dot(a, wd_ref[...])

    @pl.when(f == 0)
    def _():
        acc_ref[...] = part

    @pl.when(f > 0)
    def _():
        acc_ref[...] += part

    @pl.when(f == NF - 1)
    def _():
        o_ref[...] = h_ref[...] + 0.5 * _rms(acc_ref[...], gpost_ref[...])


def _ffn(h, gpre, gpost, wg, wu, wd, n_tiles):
    return pl.pallas_call(
        _ffn_body,
        grid=(n_tiles, NF),
        in_specs=[
            pl.BlockSpec((TM, D), lambda i, f: (i, 0)),
            pl.BlockSpec((1, D), lambda i, f: (0, 0)),
            pl.BlockSpec((1, D), lambda i, f: (0, 0)),
            pl.BlockSpec((D, TF), lambda i, f: (0, f)),
            pl.BlockSpec((D, TF), lambda i, f: (0, f)),
            pl.BlockSpec((TF, D), lambda i, f: (f, 0)),
        ],
        out_specs=pl.BlockSpec((TM, D), lambda i, f: (i, 0)),
        out_shape=jax.ShapeDtypeStruct((n_tiles * TM, D), F32),
        scratch_shapes=[pltpu.VMEM((TM, D), BF16), pltpu.VMEM((TM, D), F32)],
        compiler_params=_params(("parallel", "arbitrary"), 48),
        name="ffn",
    )(h, gpre, gpost, wg, wu, wd)


def _oproj_body(o_ref, w_ref, h_ref, g_ref, out_ref):
    m = _dot(o_ref[...], w_ref[...])
    out_ref[...] = h_ref[...] + _rms(m, g_ref[...])


def _oproj(o, w, h, g):
    return pl.pallas_call(
        _oproj_body,
        grid=(N_T,),
        in_specs=[
            pl.BlockSpec((TM, D), lambda i: (i, 0)),
            pl.BlockSpec((D, D), lambda i: (0, 0)),
            pl.BlockSpec((TM, D), lambda i: (i, 0)),
            pl.BlockSpec((1, D), lambda i: (0, 0)),
        ],
        out_specs=pl.BlockSpec((TM, D), lambda i: (i, 0)),
        out_shape=jax.ShapeDtypeStruct((M_PAD, D), F32),
        compiler_params=_params(("parallel",), 48),
        name="oproj",
    )(o, w, h, g)


def _na_proj_body(h_ref, g_ref, w_ref, o_ref, xn_ref):
    n = pl.program_id(1)

    @pl.when(n == 0)
    def _():
        xn_ref[...] = _rms(h_ref[...], g_ref[...]).astype(BF16)

    scale = jnp.where(n == 0, DH ** -0.5, 1.0).astype(F32)
    o_ref[...] = (_dot(xn_ref[...], w_ref[...]) * scale).astype(BF16)


def _na_proj(h, g, w):
    return pl.pallas_call(
        _na_proj_body,
        grid=(N_T, 3),
        in_specs=[
            pl.BlockSpec((TM, D), lambda i, n: (i, 0)),
            pl.BlockSpec((1, D), lambda i, n: (0, 0)),
            pl.BlockSpec((D, D), lambda i, n: (0, n)),
        ],
        out_specs=pl.BlockSpec((TM, D), lambda i, n: (i, n)),
        out_shape=jax.ShapeDtypeStruct((M_PAD, 3 * D), BF16),
        scratch_shapes=[pltpu.VMEM((TM, D), BF16)],
        compiler_params=_params(("parallel", "arbitrary"), 48),
        name="na_proj",
    )(h, g, w)


def _softmax_pv(scores, values):
    m = None
    for s, _ in zip(scores, values):
        mi = jnp.max(s, axis=-1, keepdims=True)
        m = mi if m is None else jnp.maximum(m, mi)
    l = None
    o = None
    for s, v in zip(scores, values):
        p = jnp.exp(s - m)
        li = jnp.sum(p, axis=-1, keepdims=True)
        oi = _dot(p.astype(BF16), v)
        l = li if l is None else l + li
        o = oi if o is None else o + oi
    return o / l


def _na_body(q_ref, k_ref, v_ref, qm_ref, km_ref, vm_ref, bias_ref, mb_ref, o_ref, om_ref):
    r = pl.program_id(1)
    rs = jnp.clip(r - WIN_R // 2, 0, ROWS - WIN_R)
    start = pl.multiple_of(rs * GRID_W, GRID_W)
    lo = lax.broadcasted_iota(jnp.int32, (GRID_W, LANES), 1) < DH

    for s in range(N_SLABS):
        sl = slice(s * LANES, (s + 1) * LANES)
        q128 = q_ref[:, sl]
        kb = k_ref[pl.ds(start, BAND), sl]
        vb = v_ref[pl.ds(start, BAND), sl]
        km = km_ref[:, sl]
        vm = vm_ref[:, sl]
        outs = []
        for half in range(2):
            hd = 2 * s + half
            qh = jnp.where(lo if half == 0 else jnp.logical_not(lo), q128, jnp.zeros_like(q128))
            sc = _dot_nt(qh, kb) + bias_ref[hd]
            sm = _dot_nt(qh, km) + mb_ref[hd:hd + 1, :]
            outs.append(_softmax_pv([sm, sc], [vm, vb]))
        o_ref[:, sl] = jnp.where(lo, outs[0], outs[1]).astype(BF16)

    @pl.when(r == 0)
    def _():
        lo_m = lax.broadcasted_iota(jnp.int32, (N_META, LANES), 1) < DH
        for s in range(N_SLABS):
            sl = slice(s * LANES, (s + 1) * LANES)
            q128 = qm_ref[:, sl]
            km = km_ref[:, sl]
            vm = vm_ref[:, sl]
            outs = []
            for half in range(2):
                qh = jnp.where(lo_m if half == 0 else jnp.logical_not(lo_m), q128, jnp.zeros_like(q128))
                outs.append(_softmax_pv([_dot_nt(qh, km)], [vm]))
            om_ref[:, sl] = jnp.where(lo_m, outs[0], outs[1]).astype(BF16)


def _na_bias_row(r):
    rs = jnp.clip(r - WIN_R // 2, 0, ROWS - WIN_R)
    return rs - r + (WIN_R - 1)


def _na_attn(qkv, bias, mbias):
    meta_blk = M_X // N_META
    return pl.pallas_call(
        _na_body,
        grid=(B, ROWS),
        in_specs=[
            pl.BlockSpec((GRID_W, D), lambda b, r: (b * ROWS + r, 0)),
            pl.BlockSpec((SEQ, D), lambda b, r: (b, 1)),
            pl.BlockSpec((SEQ, D), lambda b, r: (b, 2)),
            pl.BlockSpec((N_META, D), lambda b, r: (meta_blk + b, 0)),
            pl.BlockSpec((N_META, D), lambda b, r: (meta_blk + b, 1)),
            pl.BlockSpec((N_META, D), lambda b, r: (meta_blk + b, 2)),
            pl.BlockSpec((HEADS, None, GRID_W, BAND), lambda b, r: (0, _na_bias_row(r), 0, 0)),
            pl.BlockSpec((HEADS, N_META), lambda b, r: (0, 0)),
        ],
        out_specs=[
            pl.BlockSpec((GRID_W, D), lambda b, r: (b * ROWS + r, 0)),
            pl.BlockSpec((N_META, D), lambda b, r: (b, 0)),
        ],
        out_shape=[
            jax.ShapeDtypeStruct((M_PAD, D), BF16),
            jax.ShapeDtypeStruct((M_META, D), BF16),
        ],
        compiler_params=_params(("parallel", "arbitrary"), 56),
        name="na_attn",
    )(qkv, qkv, qkv, qkv, qkv, qkv, bias, mbias)


def _na_bias_table(rpb):
    cols = np.arange(GRID_W)
    col_start = np.clip(cols - WIN_C // 2, 0, GRID_W - WIN_C)
    kc = np.arange(GRID_W)
    valid = (kc[None, :] >= col_start[:, None]) & (kc[None, :] < col_start[:, None] + WIN_C)
    col_rel = np.clip(kc[None, :] - cols[:, None] + (WIN_C - 1), 0, 2 * WIN_C - 2)
    row_rel = np.arange(WIN_R)[:, None] + np.arange(WIN_R)[None, :]
    t = rpb.astype(F32)[:, row_rel[:, :, None, None], col_rel[None, None, :, :]]
    t = jnp.where(valid[None, None, None], t, NEG)
    return jnp.transpose(t, (0, 1, 3, 2, 4)).reshape(HEADS, WIN_R, GRID_W, BAND)


_Q_HEAD_ORDER = np.array([8 * j + 4 * hi + i for j in range(2) for i in range(4) for hi in range(2)])


def _head_norm_rope(x, gain, cos, sin_a, sin_b, lo):
    sq = x * x
    s_lo = jnp.sum(jnp.where(lo, sq, 0.0), axis=-1, keepdims=True)
    s_hi = jnp.sum(jnp.where(lo, 0.0, sq), axis=-1, keepdims=True)
    inv = lax.rsqrt(jnp.where(lo, s_lo, s_hi) * (1.0 / DH) + EPS)
    y = x * inv * gain
    quarter = DH // 4
    return y * cos + pltpu.roll(y, LANES - quarter, 1) * sin_a + pltpu.roll(y, quarter, 1) * sin_b


def _gqa_proj_body(h_ref, g_ref, wq_ref, wk_ref, wvt_ref, qg_ref, kg_ref, cos_ref, sa_ref, sb_ref,
                   q_ref, k_ref, vt_ref):
    xn = _rms(h_ref[...], g_ref[...]).astype(BF16)
    lo = lax.broadcasted_iota(jnp.int32, (TM, LANES), 1) < DH
    cos, sa, sb = cos_ref[...], sa_ref[...], sb_ref[...]
    for s in range(N_SLABS):
        sl = slice(s * LANES, (s + 1) * LANES)
        q = _head_norm_rope(_dot(xn, wq_ref[:, sl]), qg_ref[...], cos, sa, sb, lo)
        q_ref[:, sl] = (q * DH ** -0.5).astype(BF16)
    for s in range(KV_HEADS * DH // LANES):
        sl = slice(s * LANES, (s + 1) * LANES)
        k = _head_norm_rope(_dot(xn, wk_ref[:, sl]), kg_ref[...], cos, sa, sb, lo)
        k_ref[:, sl] = k.astype(BF16)
    vt = _dot_nt(wvt_ref[...], xn).astype(BF16)
    for c in range(TM // TK):
        vt_ref[c] = vt[:, c * TK:(c + 1) * TK]


def _gqa_proj(h, g, wq, wk, wvt, qg, kg, cos, sa, sb):
    nkv = KV_HEADS * DH
    tiles_per_batch = SEQ // TM
    tab = lambda i: (jnp.where(i == N_XT, tiles_per_batch, i % tiles_per_batch), 0)
    const = lambda i: (0, 0)
    return pl.pallas_call(
        _gqa_proj_body,
        grid=(N_T,),
        in_specs=[
            pl.BlockSpec((TM, D), lambda i: (i, 0)),
            pl.BlockSpec((1, D), const),
            pl.BlockSpec((D, D), const),
            pl.BlockSpec((D, nkv), const),
            pl.BlockSpec((nkv, D), const),
            pl.BlockSpec((1, LANES), const),
            pl.BlockSpec((1, LANES), const),
            pl.BlockSpec((TM, LANES), tab),
            pl.BlockSpec((TM, LANES), tab),
            pl.BlockSpec((TM, LANES), tab),
        ],
        out_specs=[
            pl.BlockSpec((TM, D), lambda i: (i, 0)),
            pl.BlockSpec((TM, nkv), lambda i: (i, 0)),
            pl.BlockSpec((TM // TK, nkv, TK), lambda i: (i, 0, 0)),
        ],
        out_shape=[
            jax.ShapeDtypeStruct((M_PAD, D), BF16),
            jax.ShapeDtypeStruct((M_PAD, nkv), BF16),
            jax.ShapeDtypeStruct((M_PAD // TK, nkv, TK), BF16),
        ],
        compiler_params=_params(("parallel",), 48),
        name="gqa_proj",
    )(h, g, wq, wk, wvt, qg, kg, cos, sa, sb)


def _rope_tables():
    t = np.arange(SEQ)
    row = (t // GRID_W).astype(np.float32)
    col = (t % GRID_W).astype(np.float32)
    sec = DH // 2
    quarter = DH // 4
    freqs = ROPE_THETA ** (-jnp.arange(0, sec, 2, dtype=F32) / sec)
    ang = jnp.stack([row[:, None] * freqs, col[:, None] * freqs], axis=1)
    cos = jnp.cos(ang)
    sin = jnp.sin(ang)
    zero = jnp.zeros_like(sin)
    cos_h = jnp.stack([cos, cos], axis=2).reshape(SEQ, DH)
    sa_h = jnp.stack([-sin, zero], axis=2).reshape(SEQ, DH)
    sb_h = jnp.stack([zero, sin], axis=2).reshape(SEQ, DH)
    reps = LANES // DH
    pad = lambda a, v: jnp.concatenate([jnp.tile(a, (1, reps)), jnp.full((TM, LANES), v, F32)], axis=0)
    return pad(cos_h, 1.0), pad(sa_h, 0.0), pad(sb_h, 0.0)


def _gqa_body(q_ref, k_ref, km_ref, vt_ref, vtm_ref, qm_ref, o_ref, om_ref, m_ref, l_ref, acc_ref):
    lo_q = lax.broadcasted_iota(jnp.int32, (TQ, LANES), 1) < DH
    lo_rows = lax.broadcasted_iota(jnp.int32, (LANES, TQ), 0) < DH

    for s in range(N_SLABS):
        sl = slice(s * LANES, (s + 1) * LANES)
        j = s // 4
        ksl = slice(j * LANES, (j + 1) * LANES)
        q128 = q_ref[:, sl]
        outs = []
        for half in range(2):
            qh = jnp.where(lo_q if half == 0 else jnp.logical_not(lo_q), q128, jnp.zeros_like(q128))
            st = _dot_nt(km_ref[:, ksl], qh)
            m0 = jnp.max(st, axis=0, keepdims=True)
            p = jnp.exp(st - m0)
            m_ref[...] = m0
            l_ref[...] = jnp.sum(p, axis=0, keepdims=True)
            acc_ref[...] = _dot(vtm_ref[ksl, :], p.astype(BF16))

            def chunk(c, carry, qh=qh, ksl=ksl):
                off = pl.multiple_of(c * TK, TK)
                st = _dot_nt(k_ref[pl.ds(off, TK), ksl], qh)
                m_old = m_ref[...]
                m_new = jnp.maximum(m_old, jnp.max(st, axis=0, keepdims=True))
                alpha = jnp.exp(m_old - m_new)
                p = jnp.exp(st - m_new)
                m_ref[...] = m_new
                l_ref[...] = alpha * l_ref[...] + jnp.sum(p, axis=0, keepdims=True)
                acc_ref[...] = alpha * acc_ref[...] + _dot(vt_ref[c, ksl, :], p.astype(BF16))
                return carry

            lax.fori_loop(0, NKC, chunk, 0)
            outs.append(acc_ref[...] / l_ref[...])
        ot = jnp.where(lo_rows, outs[0], outs[1])
        o_ref[:, sl] = ot.T.astype(BF16)

    @pl.when(pl.program_id(1) == 0)
    def _():
        lo_m = lax.broadcasted_iota(jnp.int32, (N_META, LANES), 1) < DH
        for s in range(N_SLABS):
            sl = slice(s * LANES, (s + 1) * LANES)
            j = s // 4
            ksl = slice(j * LANES, (j + 1) * LANES)
            q128 = qm_ref[:, sl]
            outs = []
            for half in range(2):
                qh = jnp.where(lo_m if half == 0 else jnp.logical_not(lo_m), q128, jnp.zeros_like(q128))
                s_m = _dot_nt(qh, km_ref[:, ksl])
                s_x = _dot_nt(qh, k_ref[:, ksl])
                m = jnp.maximum(jnp.max(s_m, axis=-1, keepdims=True), jnp.max(s_x, axis=-1, keepdims=True))
                p_m = jnp.exp(s_m - m)
                p_x = jnp.exp(s_x - m)
                l = jnp.sum(p_m, axis=-1, keepdims=True) + jnp.sum(p_x, axis=-1, keepdims=True)
                o = _dot_nt(p_m.astype(BF16), vtm_ref[ksl, :])
                p_xb = p_x.astype(BF16)
                for c in range(NKC):
                    o = o + _dot_nt(p_xb[:, c * TK:(c + 1) * TK], vt_ref[c, ksl, :])
                outs.append(o / l)
            om_ref[:, sl] = jnp.where(lo_m, outs[0], outs[1]).astype(BF16)


def _gqa_attn(q, k, k_meta, vt, vt_meta):
    nkv = KV_HEADS * DH
    meta_blk = M_X // N_META
    return pl.pallas_call(
        _gqa_body,
        grid=(B, SEQ // TQ),
        in_specs=[
            pl.BlockSpec((TQ, D), lambda b, i: (b * (SEQ // TQ) + i, 0)),
            pl.BlockSpec((SEQ, nkv), lambda b, i: (b, 0)),
            pl.BlockSpec((None, N_META, nkv), lambda b, i: (b, 0, 0)),
            pl.BlockSpec((NKC, nkv, TK), lambda b, i: (b, 0, 0)),
            pl.BlockSpec((None, nkv, N_META), lambda b, i: (b, 0, 0)),
            pl.BlockSpec((N_META, D), lambda b, i: (meta_blk + b, 0)),
        ],
        out_specs=[
            pl.BlockSpec((TQ, D), lambda b, i: (b * (SEQ // TQ) + i, 0)),
            pl.BlockSpec((N_META, D), lambda b, i: (b, 0)),
        ],
        out_shape=[
            jax.ShapeDtypeStruct((M_PAD, D), BF16),
            jax.ShapeDtypeStruct((M_META, D), BF16),
        ],
        scratch_shapes=[
            pltpu.VMEM((1, TQ), F32),
            pltpu.VMEM((1, TQ), F32),
            pltpu.VMEM((LANES, TQ), F32),
        ],
        compiler_params=_params(("parallel", "arbitrary"), 48),
        name="gqa_attn",
    )(q, k, k_meta, vt, vt_meta, q)


def _with_meta_rows(o, o_meta):
    tail = jnp.pad(o_meta, ((0, TM - M_META), (0, 0)))
    return lax.dynamic_update_slice(o, tail, (M_X, 0))


@jax.jit
def kernel(x, meta_tokens, norm_gains, ffn_w_gate, ffn_w_up, ffn_w_down, na_w_qkv, na_w_o, na_rpb, na_meta_bias,
           gqa_w_qkv, gqa_w_o, gqa_q_gain, gqa_k_gain):
    meta = jnp.broadcast_to(meta_tokens[None].astype(F32), (B, N_META, D)).reshape(M_META, D)
    h = jnp.concatenate([x.reshape(M_X, D), meta, jnp.zeros((TM - M_META, D), F32)], axis=0)
    gains = norm_gains.astype(F32).reshape(DEPTH, 6, 1, D)
    wg = ffn_w_gate.astype(BF16)
    wu = ffn_w_up.astype(BF16)
    wd = ffn_w_down.astype(BF16)
    nq = HEADS * DH
    nkv = KV_HEADS * DH

    for i in range(DEPTH):
        g = gains[i]
        j = i // 2
        h = _ffn(h, g[0], g[1], wg[i, 0], wu[i, 0], wd[i, 0], N_T)
        if i % 2 == 0:
            qkv = _na_proj(h, g[2], na_w_qkv[j].astype(BF16))
            o, o_meta = _na_attn(qkv, _na_bias_table(na_rpb[j]), na_meta_bias[j].astype(F32))
            w_o = na_w_o[j].astype(BF16)
        else:
            w = gqa_w_qkv[j]
            wq = w[:, :nq].reshape(D, HEADS, DH)[:, _Q_HEAD_ORDER].reshape(D, nq).astype(BF16)
            wk = w[:, nq:nq + nkv].astype(BF16)
            wvt = w[:, nq + nkv:].T.astype(BF16)
            qg = jnp.tile(gqa_q_gain[j].astype(F32), LANES // DH).reshape(1, LANES)
            kg = jnp.tile(gqa_k_gain[j].astype(F32), LANES // DH).reshape(1, LANES)
            cos, sa, sb = _rope_tables()
            q, k, vt = _gqa_proj(h, g[2], wq, wk, wvt, qg, kg, cos, sa, sb)
            k_meta = k[M_X:M_X + M_META].reshape(B, N_META, nkv)
            vt_meta = vt[M_X // TK].reshape(nkv, TK)[:, :M_META].reshape(nkv, B, N_META).transpose(1, 0, 2)
            o, o_meta = _gqa_attn(q, k, k_meta, vt, vt_meta)
            w_o = gqa_w_o[j].reshape(HEADS, DH, D)[_Q_HEAD_ORDER].reshape(nq, D).astype(BF16)
        h = _oproj(_with_meta_rows(o, o_meta), w_o, h, g[3])
        n_tiles = N_XT if i == DEPTH - 1 else N_T
        h = _ffn(h, g[4], g[5], wg[i, 1], wu[i, 1], wd[i, 1], n_tiles)
    return h.reshape(B, SEQ, D)
```

```python
import math

import numpy as np
import jax
import jax.numpy as jnp
from jax import lax
from jax.experimental import pallas as pl
from jax.experimental.pallas import tpu as pltpu

D = 1024
B = 16
SEQ = 4096
DEPTH = 2
GRID_W = 64
ROWS = SEQ // GRID_W
N_META = 16
HEADS = 16
DH = 64
WIN_R = 8
WIN_C = 16
KV_HEADS = 4
NKV = KV_HEADS * DH
D_FF = 2816
EPS = 1e-6
ROPE_THETA = 10000.0

LANES = 128
N_SLABS = D // LANES
M_X = B * SEQ
M_META = B * N_META
TM = 1024
N_XT = M_X // TM
M_PAD = M_X + TM
N_T = M_PAD // TM
TF = 256
NF = D_FF // TF
BAND = WIN_R * GRID_W
TQ = 512
TK = 512
NKC = SEQ // TK
VROWS = DH + 16
NEG = -1e30
LOG2E = math.log2(math.e)

F32 = jnp.float32
BF16 = jnp.bfloat16


def _dot(a, b):
    return jnp.dot(a, b, preferred_element_type=F32)


def _dot_nt(a, b):
    return lax.dot_general(a, b, (((1,), (1,)), ((), ())), preferred_element_type=F32)


def _params(sem, vmem_mib):
    return pltpu.CompilerParams(dimension_semantics=sem, vmem_limit_bytes=vmem_mib * 2 ** 20)


def _rms(x, gain):
    ms = jnp.mean(x * x, axis=-1, keepdims=True)
    return x * lax.rsqrt(ms + EPS) * gain


def _split_heads(q128, lo):
    zero = jnp.zeros_like(q128)
    return jnp.concatenate([jnp.where(lo, q128, zero), jnp.where(lo, zero, q128)], axis=0)


def _ffn_body(h_ref, gpre_ref, gpost_ref, wg_ref, wu_ref, wd_ref, o_ref, xn_ref, acc_ref):
    f = pl.program_id(1)

    @pl.when(f == 0)
    def _():
        xn_ref[...] = _rms(h_ref[...], gpre_ref[...]).astype(BF16)

    xn = xn_ref[...]
    g = _dot(xn, wg_ref[...])
    u = _dot(xn, wu_ref[...])
    a = (g * jax.nn.sigmoid(g) * u).astype(BF16)
    part = _dot(a, wd_ref[...])

    @pl.when(f == 0)
    def _():
        acc_ref[...] = part

    @pl.when(f > 0)
    def _():
        acc_ref[...] += part

    @pl.when(f == NF - 1)
    def _():
        o_ref[...] = h_ref[...] + 0.5 * _rms(acc_ref[...], gpost_ref[...])


def _ffn(h, gpre, gpost, wg, wu, wd, n_tiles):
    return pl.pallas_call(
        _ffn_body,
        grid=(n_tiles, NF),
        in_specs=[
            pl.BlockSpec((TM, D), lambda i, f: (i, 0)),
            pl.BlockSpec((1, D), lambda i, f: (0, 0)),
            pl.BlockSpec((1, D), lambda i, f: (0, 0)),
            pl.BlockSpec((D, TF), lambda i, f: (0, f)),
            pl.BlockSpec((D, TF), lambda i, f: (0, f)),
            pl.BlockSpec((TF, D), lambda i, f: (f, 0)),
        ],
        out_specs=pl.BlockSpec((TM, D), lambda i, f: (i, 0)),
        out_shape=jax.ShapeDtypeStruct((n_tiles * TM, D), F32),
        scratch_shapes=[pltpu.VMEM((TM, D), BF16), pltpu.VMEM((TM, D), F32)],
        compiler_params=_params(("parallel", "arbitrary"), 48),
        name="ffn",
    )(h, gpre, gpost, wg, wu, wd)


def _oproj_body(o_ref, w_ref, h_ref, g_ref, out_ref):
    m = _dot(o_ref[...], w_ref[...])
    out_ref[...] = h_ref[...] + _rms(m, g_ref[...])


def _oproj(o, w, h, g):
    return pl.pallas_call(
        _oproj_body,
        grid=(N_T,),
        in_specs=[
            pl.BlockSpec((TM, D), lambda i: (i, 0)),
            pl.BlockSpec((D, D), lambda i: (0, 0)),
            pl.BlockSpec((TM, D), lambda i: (i, 0)),
            pl.BlockSpec((1, D), lambda i: (0, 0)),
        ],
        out_specs=pl.BlockSpec((TM, D), lambda i: (i, 0)),
        out_shape=jax.ShapeDtypeStruct((M_PAD, D), F32),
        compiler_params=_params(("parallel",), 48),
        name="oproj",
    )(o, w, h, g)


def _na_proj_body(h_ref, g_ref, w_ref, o_ref, xn_ref):
    n = pl.program_id(1)

    @pl.when(n == 0)
    def _():
        xn_ref[...] = _rms(h_ref[...], g_ref[...]).astype(BF16)

    scale = jnp.where(n == 0, DH ** -0.5, 1.0).astype(F32)
    o_ref[...] = (_dot(xn_ref[...], w_ref[...]) * scale).astype(BF16)


def _na_proj(h, g, w):
    return pl.pallas_call(
        _na_proj_body,
        grid=(N_T, 3),
        in_specs=[
            pl.BlockSpec((TM, D), lambda i, n: (i, 0)),
            pl.BlockSpec((1, D), lambda i, n: (0, 0)),
            pl.BlockSpec((D, D), lambda i, n: (0, n)),
        ],
        out_specs=pl.BlockSpec((TM, D), lambda i, n: (i, n)),
        out_shape=jax.ShapeDtypeStruct((M_PAD, 3 * D), BF16),
        scratch_shapes=[pltpu.VMEM((TM, D), BF16)],
        compiler_params=_params(("parallel", "arbitrary"), 48),
        name="na_proj",
    )(h, g, w)


def _softmax_pv(scores, values):
    m = None
    for s in scores:
        mi = jnp.max(s, axis=-1, keepdims=True)
        m = mi if m is None else jnp.maximum(m, mi)
    l = None
    o = None
    for s, v in zip(scores, values):
        p = jnp.exp(s - m)
        li = jnp.sum(p, axis=-1, keepdims=True)
        oi = _dot(p.astype(BF16), v)
        l = li if l is None else l + li
        o = oi if o is None else o + oi
    return o / l


def _na_body(q_ref, k_ref, v_ref, qm_ref, km_ref, vm_ref, bias_ref, mb_ref, o_ref, om_ref):
    r = pl.program_id(1)
    rs = jnp.clip(r - WIN_R // 2, 0, ROWS - WIN_R)
    start = pl.multiple_of(rs * GRID_W, GRID_W)
    lo = lax.broadcasted_iota(jnp.int32, (GRID_W, LANES), 1) < DH

    def scores(s):
        sl = slice(s * LANES, (s + 1) * LANES)
        qs = _split_heads(q_ref[:, sl], lo)
        sc = _dot_nt(qs, k_ref[pl.ds(start, BAND), sl]) + bias_ref[2 * s:2 * s + 2].reshape(2 * GRID_W, BAND)
        sm = _dot_nt(qs, km_ref[:, sl]) + mb_ref[s]
        return sm, sc

    pending = scores(0)
    for s in range(N_SLABS):
        sl = slice(s * LANES, (s + 1) * LANES)
        sm, sc = pending
        if s + 1 < N_SLABS:
            pending = scores(s + 1)
        o = _softmax_pv([sm, sc], [vm_ref[:, sl], v_ref[pl.ds(start, BAND), sl]])
        o_ref[:, sl] = jnp.where(lo, o[:GRID_W], o[GRID_W:]).astype(BF16)

    @pl.when(r == 0)
    def _():
        lo_m = lax.broadcasted_iota(jnp.int32, (N_META, LANES), 1) < DH
        for s in range(N_SLABS):
            sl = slice(s * LANES, (s + 1) * LANES)
            qs = _split_heads(qm_ref[:, sl], lo_m)
            o = _softmax_pv([_dot_nt(qs, km_ref[:, sl])], [vm_ref[:, sl]])
            om_ref[:, sl] = jnp.where(lo_m, o[:N_META], o[N_META:]).astype(BF16)


def _na_bias_row(r):
    rs = jnp.clip(r - WIN_R // 2, 0, ROWS - WIN_R)
    return rs - r + (WIN_R - 1)


def _na_attn(qkv, bias, mbias):
    meta_blk = M_X // N_META
    return pl.pallas_call(
        _na_body,
        grid=(B, ROWS),
        in_specs=[
            pl.BlockSpec((GRID_W, D), lambda b, r: (b * ROWS + r, 0)),
            pl.BlockSpec((SEQ, D), lambda b, r: (b, 1)),
            pl.BlockSpec((SEQ, D), lambda b, r: (b, 2)),
            pl.BlockSpec((N_META, D), lambda b, r: (meta_blk + b, 0)),
            pl.BlockSpec((N_META, D), lambda b, r: (meta_blk + b, 1)),
            pl.BlockSpec((N_META, D), lambda b, r: (meta_blk + b, 2)),
            pl.BlockSpec((HEADS, None, GRID_W, BAND), lambda b, r: (0, _na_bias_row(r), 0, 0)),
            pl.BlockSpec((N_SLABS, 2 * GRID_W, N_META), lambda b, r: (0, 0, 0)),
        ],
        out_specs=[
            pl.BlockSpec((GRID_W, D), lambda b, r: (b * ROWS + r, 0)),
            pl.BlockSpec((N_META, D), lambda b, r: (b, 0)),
        ],
        out_shape=[
            jax.ShapeDtypeStruct((M_PAD, D), BF16),
            jax.ShapeDtypeStruct((M_META, D), BF16),
        ],
        compiler_params=_params(("parallel", "arbitrary"), 56),
        name="na_attn",
    )(qkv, qkv, qkv, qkv, qkv, qkv, bias, mbias)


def _na_bias_table(rpb):
    cols = np.arange(GRID_W)
    col_start = np.clip(cols - WIN_C // 2, 0, GRID_W - WIN_C)
    kc = np.arange(GRID_W)
    valid = (kc[None, :] >= col_start[:, None]) & (kc[None, :] < col_start[:, None] + WIN_C)
    col_rel = kc[None, :] - cols[:, None] + (WIN_C - 1)
    onehot = np.zeros((GRID_W, GRID_W, 2 * WIN_C - 1), np.float32)
    ci, ki = np.nonzero(valid)
    onehot[ci, ki, col_rel[ci, ki]] = 1.0
    base = jnp.einsum("hrx,ckx->hrck", rpb.astype(F32), onehot, precision=lax.Precision.HIGHEST)
    base = jnp.where(valid[None, None], base, NEG)
    t = jnp.stack([base[:, d:d + WIN_R] for d in range(WIN_R)], axis=1)
    return jnp.transpose(t, (0, 1, 3, 2, 4)).reshape(HEADS, WIN_R, GRID_W, BAND)


def _na_meta_bias_table(meta_bias):
    mb = meta_bias.astype(F32).reshape(N_SLABS, 2, 1, N_META)
    return jnp.broadcast_to(mb, (N_SLABS, 2, GRID_W, N_META)).reshape(N_SLABS, 2 * GRID_W, N_META)


_Q_HEAD_ORDER = np.array([8 * j + 4 * hi + i for j in range(2) for i in range(4) for hi in range(2)])


def _head_norm_rope(x, gain, cos, sin_a, sin_b, lo):
    sq = x * x
    s_lo = jnp.sum(jnp.where(lo, sq, 0.0), axis=-1, keepdims=True)
    s_hi = jnp.sum(jnp.where(lo, 0.0, sq), axis=-1, keepdims=True)
    inv = lax.rsqrt(jnp.where(lo, s_lo, s_hi) * (1.0 / DH) + EPS)
    y = x * inv * gain
    quarter = DH // 4
    return y * cos + pltpu.roll(y, LANES - quarter, 1) * sin_a + pltpu.roll(y, quarter, 1) * sin_b


def _gqa_proj_body(h_ref, g_ref, wq_ref, wk_ref, wv_ref, wvt_ref, qg_ref, kg_ref, cos_ref, sa_ref, sb_ref,
                   q_ref, k_ref, v_ref, vt_ref):
    xn = _rms(h_ref[...], g_ref[...]).astype(BF16)
    lo = lax.broadcasted_iota(jnp.int32, (TM, LANES), 1) < DH
    cos, sa, sb = cos_ref[...], sa_ref[...], sb_ref[...]
    for s in range(N_SLABS):
        sl = slice(s * LANES, (s + 1) * LANES)
        q = _head_norm_rope(_dot(xn, wq_ref[:, sl]), qg_ref[...], cos, sa, sb, lo)
        q_ref[:, sl] = (q * (DH ** -0.5 * LOG2E)).astype(BF16)
    for s in range(NKV // LANES):
        sl = slice(s * LANES, (s + 1) * LANES)
        k = _head_norm_rope(_dot(xn, wk_ref[:, sl]), kg_ref[...], cos, sa, sb, lo)
        k_ref[:, sl] = k.astype(BF16)
    v_ref[...] = _dot(xn, wv_ref[...]).astype(BF16)
    vt = _dot_nt(wvt_ref[...], xn).astype(BF16)
    ones = jnp.ones((VROWS - DH, TK), BF16)
    for c in range(TM // TK):
        for g in range(KV_HEADS):
            vt_ref[c, g * VROWS:g * VROWS + DH, :] = vt[g * DH:(g + 1) * DH, c * TK:(c + 1) * TK]
            vt_ref[c, g * VROWS + DH:(g + 1) * VROWS, :] = ones


def _gqa_proj(h, g, wq, wk, wv, wvt, qg, kg, cos, sa, sb):
    tiles_per_batch = SEQ // TM
    tab = lambda i: (jnp.where(i == N_XT, tiles_per_batch, i % tiles_per_batch), 0)
    const = lambda i: (0, 0)
    return pl.pallas_call(
        _gqa_proj_body,
        grid=(N_T,),
        in_specs=[
            pl.BlockSpec((TM, D), lambda i: (i, 0)),
            pl.BlockSpec((1, D), const),
            pl.BlockSpec((D, D), const),
            pl.BlockSpec((D, NKV), const),
            pl.BlockSpec((D, NKV), const),
            pl.BlockSpec((NKV, D), const),
            pl.BlockSpec((1, LANES), const),
            pl.BlockSpec((1, LANES), const),
            pl.BlockSpec((TM, LANES), tab),
            pl.BlockSpec((TM, LANES), tab),
            pl.BlockSpec((TM, LANES), tab),
        ],
        out_specs=[
            pl.BlockSpec((TM, D), lambda i: (i, 0)),
            pl.BlockSpec((TM, NKV), lambda i: (i, 0)),
            pl.BlockSpec((TM, NKV), lambda i: (i, 0)),
            pl.BlockSpec((TM // TK, KV_HEADS * VROWS, TK), lambda i: (i, 0, 0)),
        ],
        out_shape=[
            jax.ShapeDtypeStruct((M_PAD, D), BF16),
            jax.ShapeDtypeStruct((M_PAD, NKV), BF16),
            jax.ShapeDtypeStruct((M_PAD, NKV), BF16),
            jax.ShapeDtypeStruct((M_PAD // TK, KV_HEADS * VROWS, TK), BF16),
        ],
        compiler_params=_params(("parallel",), 48),
        name="gqa_proj",
    )(h, g, wq, wk, wv, wvt, qg, kg, cos, sa, sb)


def _rope_tables():
    t = np.arange(SEQ)
    row = (t // GRID_W).astype(np.float32)
    col = (t % GRID_W).astype(np.float32)
    sec = DH // 2
    freqs = ROPE_THETA ** (-jnp.arange(0, sec, 2, dtype=F32) / sec)
    ang = jnp.stack([row[:, None] * freqs, col[:, None] * freqs], axis=1)
    cos = jnp.cos(ang)
    sin = jnp.sin(ang)
    zero = jnp.zeros_like(sin)
    cos_h = jnp.stack([cos, cos], axis=2).reshape(SEQ, DH)
    sa_h = jnp.stack([-sin, zero], axis=2).reshape(SEQ, DH)
    sb_h = jnp.stack([zero, sin], axis=2).reshape(SEQ, DH)
    reps = LANES // DH
    pad = lambda a, v: jnp.concatenate([jnp.tile(a, (1, reps)), jnp.full((TM, LANES), v, F32)], axis=0)
    return pad(cos_h, 1.0), pad(sa_h, 0.0), pad(sb_h, 0.0)


def _gqa_body(q_ref, k_ref, km_ref, vt_ref, vtm_ref, o_ref, qh_ref, m_ref, acc_ref):
    lo_q = lax.broadcasted_iota(jnp.int32, (TQ, LANES), 1) < DH

    def head_slices(hd):
        s, half = divmod(hd, 2)
        j = s // 4
        g = 2 * j + half
        return slice(j * LANES, (j + 1) * LANES), slice(g * VROWS, (g + 1) * VROWS)

    for s in range(N_SLABS):
        qs = _split_heads(q_ref[:, s * LANES:(s + 1) * LANES], lo_q)
        for half in range(2):
            hd = 2 * s + half
            ksl, vsl = head_slices(hd)
            qh = qs[half * TQ:(half + 1) * TQ]
            qh_ref[hd] = qh
            st = _dot_nt(km_ref[:, ksl], qh)
            m0 = jnp.max(st, axis=0, keepdims=True)
            m_ref[hd] = m0
            acc_ref[hd] = _dot(vtm_ref[vsl, :], jnp.exp2(st - m0).astype(BF16))

    def chunk(c, carry):
        off = pl.multiple_of(c * TK, TK)

        def scores(hd):
            ksl, _ = head_slices(hd)
            return _dot_nt(k_ref[pl.ds(off, TK), ksl], qh_ref[hd])

        pending = scores(0)
        for hd in range(HEADS):
            st = pending
            if hd + 1 < HEADS:
                pending = scores(hd + 1)
            _, vsl = head_slices(hd)
            m_old = m_ref[hd]
            m_new = jnp.maximum(m_old, jnp.max(st, axis=0, keepdims=True))
            m_ref[hd] = m_new
            p = jnp.exp2(st - m_new).astype(BF16)
            acc_ref[hd] = jnp.exp2(m_old - m_new) * acc_ref[hd] + _dot(vt_ref[c, vsl, :], p)
        return carry

    lax.fori_loop(0, NKC, chunk, 0)

    for s in range(N_SLABS):
        a = acc_ref[2 * s]
        b = acc_ref[2 * s + 1]
        ot = jnp.concatenate([a[:DH] / a[DH:DH + 1], b[:DH] / b[DH:DH + 1]], axis=0)
        o_ref[:, s * LANES:(s + 1) * LANES] = ot.T.astype(BF16)


def _gqa_attn(q, k, k_meta, vt, vt_meta):
    return pl.pallas_call(
        _gqa_body,
        grid=(B, SEQ // TQ),
        in_specs=[
            pl.BlockSpec((TQ, D), lambda b, i: (b * (SEQ // TQ) + i, 0)),
            pl.BlockSpec((SEQ, NKV), lambda b, i: (b, 0)),
            pl.BlockSpec((None, N_META, NKV), lambda b, i: (b, 0, 0)),
            pl.BlockSpec((NKC, KV_HEADS * VROWS, TK), lambda b, i: (b, 0, 0)),
            pl.BlockSpec((None, KV_HEADS * VROWS, N_META), lambda b, i: (b, 0, 0)),
        ],
        out_specs=pl.BlockSpec((TQ, D), lambda b, i: (b * (SEQ // TQ) + i, 0)),
        out_shape=jax.ShapeDtypeStruct((M_PAD, D), BF16),
        scratch_shapes=[
            pltpu.VMEM((HEADS, TQ, LANES), BF16),
            pltpu.VMEM((HEADS, 1, TQ), F32),
            pltpu.VMEM((HEADS, VROWS, TQ), F32),
        ],
        compiler_params=_params(("parallel", "arbitrary"), 48),
        name="gqa_attn",
    )(q, k, k_meta, vt, vt_meta)


def _gqa_meta_body(qm_ref, k_ref, km_ref, v_ref, vm_ref, om_ref):
    lo_m = lax.broadcasted_iota(jnp.int32, (N_META, LANES), 1) < DH
    slabs_per_pair = N_SLABS // (NKV // LANES)
    for j in range(NKV // LANES):
        ksl = slice(j * LANES, (j + 1) * LANES)
        qs = jnp.concatenate(
            [_split_heads(qm_ref[:, s * LANES:(s + 1) * LANES], lo_m)
             for s in range(j * slabs_per_pair, (j + 1) * slabs_per_pair)], axis=0)
        s_m = _dot_nt(qs, km_ref[:, ksl])
        s_x = _dot_nt(qs, k_ref[:, ksl])
        m = jnp.maximum(jnp.max(s_m, axis=-1, keepdims=True), jnp.max(s_x, axis=-1, keepdims=True))
        p_m = jnp.exp2(s_m - m)
        p_x = jnp.exp2(s_x - m)
        l = jnp.sum(p_m, axis=-1, keepdims=True) + jnp.sum(p_x, axis=-1, keepdims=True)
        o = (_dot(p_m.astype(BF16), vm_ref[:, ksl]) + _dot(p_x.astype(BF16), v_ref[:, ksl])) / l
        for i in range(slabs_per_pair):
            s = j * slabs_per_pair + i
            r0 = 2 * N_META * i
            om_ref[:, s * LANES:(s + 1) * LANES] = jnp.where(
                lo_m, o[r0:r0 + N_META], o[r0 + N_META:r0 + 2 * N_META]).astype(BF16)


def _gqa_meta_attn(q, k, v):
    meta_blk = M_X // N_META
    return pl.pallas_call(
        _gqa_meta_body,
        grid=(B,),
        in_specs=[
            pl.BlockSpec((N_META, D), lambda b: (meta_blk + b, 0)),
            pl.BlockSpec((SEQ, NKV), lambda b: (b, 0)),
            pl.BlockSpec((N_META, NKV), lambda b: (meta_blk + b, 0)),
            pl.BlockSpec((SEQ, NKV), lambda b: (b, 0)),
            pl.BlockSpec((N_META, NKV), lambda b: (meta_blk + b, 0)),
        ],
        out_specs=pl.BlockSpec((N_META, D), lambda b: (b, 0)),
        out_shape=jax.ShapeDtypeStruct((M_META, D), BF16),
        compiler_params=_params(("parallel",), 48),
        name="gqa_meta_attn",
    )(q, k, k, v, v)


def _with_meta_rows(o, o_meta):
    tail = jnp.pad(o_meta, ((0, TM - M_META), (0, 0)))
    return lax.dynamic_update_slice(o, tail, (M_X, 0))


@jax.jit
def kernel(x, meta_tokens, norm_gains, ffn_w_gate, ffn_w_up, ffn_w_down, na_w_qkv, na_w_o, na_rpb, na_meta_bias,
           gqa_w_qkv, gqa_w_o, gqa_q_gain, gqa_k_gain):
    meta = jnp.broadcast_to(meta_tokens[None].astype(F32), (B, N_META, D)).reshape(M_META, D)
    h = jnp.concatenate([x.reshape(M_X, D), meta, jnp.zeros((TM - M_META, D), F32)], axis=0)
    gains = norm_gains.astype(F32).reshape(DEPTH, 6, 1, D)
    wg = ffn_w_gate.astype(BF16)
    wu = ffn_w_up.astype(BF16)
    wd = ffn_w_down.astype(BF16)
    nq = HEADS * DH

    for i in range(DEPTH):
        g = gains[i]
        j = i // 2
        h = _ffn(h, g[0], g[1], wg[i, 0], wu[i, 0], wd[i, 0], N_T)
        if i % 2 == 0:
            qkv = _na_proj(h, g[2], na_w_qkv[j].astype(BF16))
            o, o_meta = _na_attn(qkv, _na_bias_table(na_rpb[j]), _na_meta_bias_table(na_meta_bias[j]))
            w_o = na_w_o[j].astype(BF16)
        else:
            w = gqa_w_qkv[j]
            wq = w[:, :nq].reshape(D, HEADS, DH)[:, _Q_HEAD_ORDER].reshape(D, nq).astype(BF16)
            wk = w[:, nq:nq + NKV].astype(BF16)
            wv = w[:, nq + NKV:].astype(BF16)
            qg = jnp.tile(gqa_q_gain[j].astype(F32), LANES // DH).reshape(1, LANES)
            kg = jnp.tile(gqa_k_gain[j].astype(F32), LANES // DH).reshape(1, LANES)
            cos, sa, sb = _rope_tables()
            q, k, v, vt = _gqa_proj(h, g[2], wq, wk, wv, wv.T, qg, kg, cos, sa, sb)
            k_meta = k[M_X:M_X + M_META].reshape(B, N_META, NKV)
            vt_meta = vt[M_X // TK][:, :M_META].reshape(KV_HEADS * VROWS, B, N_META).transpose(1, 0, 2)
            o = _gqa_attn(q, k, k_meta, vt, vt_meta)
            o_meta = _gqa_meta_attn(q, k, v)
            w_o = gqa_w_o[j].reshape(HEADS, DH, D)[_Q_HEAD_ORDER].reshape(nq, D).astype(BF16)
        h = _oproj(_with_meta_rows(o, o_meta), w_o, h, g[3])
        n_tiles = N_XT if i == DEPTH - 1 else N_T
        h = _ffn(h, g[4], g[5], wg[i, 1], wu[i, 1], wd[i, 1], n_tiles)
    return h.reshape(B, SEQ, D)
```

```python
import functools
import math

import numpy as np
import jax
import jax.numpy as jnp
from jax import lax
from jax.experimental import pallas as pl
from jax.experimental.pallas import tpu as pltpu

D = 1024
B = 16
SEQ = 4096
DEPTH = 2
GRID_W = 64
ROWS = SEQ // GRID_W
N_META = 16
HEADS = 16
DH = 64
WIN_R = 8
WIN_C = 16
KV_HEADS = 4
NKV = KV_HEADS * DH
D_FF = 2816
EPS = 1e-6
ROPE_THETA = 10000.0

LANES = 128
N_SLABS = D // LANES
M_X = B * SEQ
M_META = B * N_META
TM = 1024
N_XT = M_X // TM
M_PAD = M_X + TM
N_T = M_PAD // TM
TF = 256
NF = D_FF // TF
BAND = WIN_R * GRID_W
TQ = 512
TK = 512
NKC = SEQ // TK
VROWS = DH + 16
NEG = -1e30
MAX_DIRECT_EXPONENT = 40.0
LOG2E = math.log2(math.e)

F32 = jnp.float32
BF16 = jnp.bfloat16


def _dot(a, b):
    return jnp.dot(a, b, preferred_element_type=F32)


def _dot_nt(a, b):
    return lax.dot_general(a, b, (((1,), (1,)), ((), ())), preferred_element_type=F32)


def _params(sem, vmem_mib):
    return pltpu.CompilerParams(dimension_semantics=sem, vmem_limit_bytes=vmem_mib * 2 ** 20)


def _rms(x, gain):
    ms = jnp.mean(x * x, axis=-1, keepdims=True)
    return x * lax.rsqrt(ms + EPS) * gain


def _split_heads(q128, lo):
    zero = jnp.zeros_like(q128)
    return jnp.concatenate([jnp.where(lo, q128, zero), jnp.where(lo, zero, q128)], axis=0)


def _ffn_body(h_ref, gpre_ref, gpost_ref, wg_ref, wu_ref, wd_ref, o_ref, xn_ref, acc_ref):
    xn_ref[...] = _rms(h_ref[...], gpre_ref[...]).astype(BF16)

    def gate_up(f):
        xn = xn_ref[...]
        cols = slice(f * TF, (f + 1) * TF)
        return _dot(xn, wg_ref[:, cols]), _dot(xn, wu_ref[:, cols])

    pending = gate_up(0)
    for f in range(NF):
        g, u = pending
        if f + 1 < NF:
            pending = gate_up(f + 1)
        a = (g * jax.nn.sigmoid(g) * u).astype(BF16)
        part = _dot(a, wd_ref[f * TF:(f + 1) * TF, :])
        if f == 0:
            acc_ref[...] = part
        else:
            acc_ref[...] += part
    o_ref[...] = h_ref[...] + 0.5 * _rms(acc_ref[...], gpost_ref[...])


def _ffn(h, gpre, gpost, wg, wu, wd, n_tiles):
    const = lambda i: (0, 0)
    resident = pl.Buffered(1)
    return pl.pallas_call(
        _ffn_body,
        grid=(n_tiles,),
        in_specs=[
            pl.BlockSpec((TM, D), lambda i: (i, 0)),
            pl.BlockSpec((1, D), const),
            pl.BlockSpec((1, D), const),
            pl.BlockSpec((D, D_FF), const, pipeline_mode=resident),
            pl.BlockSpec((D, D_FF), const, pipeline_mode=resident),
            pl.BlockSpec((D_FF, D), const, pipeline_mode=resident),
        ],
        out_specs=pl.BlockSpec((TM, D), lambda i: (i, 0)),
        out_shape=jax.ShapeDtypeStruct((n_tiles * TM, D), F32),
        scratch_shapes=[pltpu.VMEM((TM, D), BF16), pltpu.VMEM((TM, D), F32)],
        compiler_params=_params(("parallel",), 56),
        name="ffn",
    )(h, gpre, gpost, wg, wu, wd)


def _oproj_body(o_ref, w_ref, h_ref, g_ref, out_ref):
    m = _dot(o_ref[...], w_ref[...])
    out_ref[...] = h_ref[...] + _rms(m, g_ref[...])


def _oproj(o, w, h, g):
    return pl.pallas_call(
        _oproj_body,
        grid=(N_T,),
        in_specs=[
            pl.BlockSpec((TM, D), lambda i: (i, 0)),
            pl.BlockSpec((D, D), lambda i: (0, 0)),
            pl.BlockSpec((TM, D), lambda i: (i, 0)),
            pl.BlockSpec((1, D), lambda i: (0, 0)),
        ],
        out_specs=pl.BlockSpec((TM, D), lambda i: (i, 0)),
        out_shape=jax.ShapeDtypeStruct((M_PAD, D), F32),
        compiler_params=_params(("parallel",), 48),
        name="oproj",
    )(o, w, h, g)


def _na_proj_body(h_ref, g_ref, w_ref, o_ref, xn_ref):
    n = pl.program_id(1)

    @pl.when(n == 0)
    def _():
        xn_ref[...] = _rms(h_ref[...], g_ref[...]).astype(BF16)

    scale = jnp.where(n == 0, DH ** -0.5, 1.0).astype(F32)
    o_ref[...] = (_dot(xn_ref[...], w_ref[...]) * scale).astype(BF16)


def _na_proj(h, g, w):
    return pl.pallas_call(
        _na_proj_body,
        grid=(N_T, 3),
        in_specs=[
            pl.BlockSpec((TM, D), lambda i, n: (i, 0)),
            pl.BlockSpec((1, D), lambda i, n: (0, 0)),
            pl.BlockSpec((D, D), lambda i, n: (0, n)),
        ],
        out_specs=pl.BlockSpec((TM, D), lambda i, n: (i, n)),
        out_shape=jax.ShapeDtypeStruct((M_PAD, 3 * D), BF16),
        scratch_shapes=[pltpu.VMEM((TM, D), BF16)],
        compiler_params=_params(("parallel", "arbitrary"), 48),
        name="na_proj",
    )(h, g, w)


def _softmax_pv(scores, values):
    m = None
    for s in scores:
        mi = jnp.max(s, axis=-1, keepdims=True)
        m = mi if m is None else jnp.maximum(m, mi)
    l = None
    o = None
    for s, v in zip(scores, values):
        p = jnp.exp(s - m)
        li = jnp.sum(p, axis=-1, keepdims=True)
        oi = _dot(p.astype(BF16), v)
        l = li if l is None else l + li
        o = oi if o is None else o + oi
    return o / l


def _na_body(q_ref, k_ref, v_ref, qm_ref, km_ref, vm_ref, bias_ref, mb_ref, o_ref, om_ref):
    r = pl.program_id(1)
    rs = jnp.clip(r - WIN_R // 2, 0, ROWS - WIN_R)
    start = pl.multiple_of(rs * GRID_W, GRID_W)
    lo = lax.broadcasted_iota(jnp.int32, (GRID_W, LANES), 1) < DH

    def scores(s):
        sl = slice(s * LANES, (s + 1) * LANES)
        qs = _split_heads(q_ref[:, sl], lo)
        sc = _dot_nt(qs, k_ref[pl.ds(start, BAND), sl]) + bias_ref[2 * s:2 * s + 2].reshape(2 * GRID_W, BAND)
        sm = _dot_nt(qs, km_ref[:, sl]) + mb_ref[s]
        return sm, sc

    pending = scores(0)
    for s in range(N_SLABS):
        sl = slice(s * LANES, (s + 1) * LANES)
        sm, sc = pending
        if s + 1 < N_SLABS:
            pending = scores(s + 1)
        o = _softmax_pv([sm, sc], [vm_ref[:, sl], v_ref[pl.ds(start, BAND), sl]])
        o_ref[:, sl] = jnp.where(lo, o[:GRID_W], o[GRID_W:]).astype(BF16)

    @pl.when(r == 0)
    def _():
        lo_m = lax.broadcasted_iota(jnp.int32, (N_META, LANES), 1) < DH
        for s in range(N_SLABS):
            sl = slice(s * LANES, (s + 1) * LANES)
            qs = _split_heads(qm_ref[:, sl], lo_m)
            o = _softmax_pv([_dot_nt(qs, km_ref[:, sl])], [vm_ref[:, sl]])
            om_ref[:, sl] = jnp.where(lo_m, o[:N_META], o[N_META:]).astype(BF16)


def _na_bias_row(r):
    rs = jnp.clip(r - WIN_R // 2, 0, ROWS - WIN_R)
    return rs - r + (WIN_R - 1)


def _na_attn(qkv, bias, mbias):
    meta_blk = M_X // N_META
    return pl.pallas_call(
        _na_body,
        grid=(B, ROWS),
        in_specs=[
            pl.BlockSpec((GRID_W, D), lambda b, r: (b * ROWS + r, 0)),
            pl.BlockSpec((SEQ, D), lambda b, r: (b, 1)),
            pl.BlockSpec((SEQ, D), lambda b, r: (b, 2)),
            pl.BlockSpec((N_META, D), lambda b, r: (meta_blk + b, 0)),
            pl.BlockSpec((N_META, D), lambda b, r: (meta_blk + b, 1)),
            pl.BlockSpec((N_META, D), lambda b, r: (meta_blk + b, 2)),
            pl.BlockSpec((HEADS, None, GRID_W, BAND), lambda b, r: (0, _na_bias_row(r), 0, 0)),
            pl.BlockSpec((N_SLABS, 2 * GRID_W, N_META), lambda b, r: (0, 0, 0)),
        ],
        out_specs=[
            pl.BlockSpec((GRID_W, D), lambda b, r: (b * ROWS + r, 0)),
            pl.BlockSpec((N_META, D), lambda b, r: (b, 0)),
        ],
        out_shape=[
            jax.ShapeDtypeStruct((M_PAD, D), BF16),
            jax.ShapeDtypeStruct((M_META, D), BF16),
        ],
        compiler_params=_params(("parallel", "arbitrary"), 56),
        name="na_attn",
    )(qkv, qkv, qkv, qkv, qkv, qkv, bias, mbias)


def _na_bias_table(rpb):
    cols = np.arange(GRID_W)
    col_start = np.clip(cols - WIN_C // 2, 0, GRID_W - WIN_C)
    kc = np.arange(GRID_W)
    valid = (kc[None, :] >= col_start[:, None]) & (kc[None, :] < col_start[:, None] + WIN_C)
    col_rel = kc[None, :] - cols[:, None] + (WIN_C - 1)
    onehot = np.zeros((GRID_W, GRID_W, 2 * WIN_C - 1), np.float32)
    ci, ki = np.nonzero(valid)
    onehot[ci, ki, col_rel[ci, ki]] = 1.0
    base = jnp.einsum("hrx,ckx->hrck", rpb.astype(F32), onehot, precision=lax.Precision.HIGHEST)
    base = jnp.where(valid[None, None], base, NEG)
    t = jnp.stack([base[:, d:d + WIN_R] for d in range(WIN_R)], axis=1)
    return jnp.transpose(t, (0, 1, 3, 2, 4)).reshape(HEADS, WIN_R, GRID_W, BAND)


def _na_meta_bias_table(meta_bias):
    mb = meta_bias.astype(F32).reshape(N_SLABS, 2, 1, N_META)
    return jnp.broadcast_to(mb, (N_SLABS, 2, GRID_W, N_META)).reshape(N_SLABS, 2 * GRID_W, N_META)


_Q_HEAD_ORDER = np.array([8 * j + 4 * hi + i for j in range(2) for i in range(4) for hi in range(2)])


def _head_norm_rope(x, gain, cos, sin_a, sin_b, lo):
    sq = x * x
    s_lo = jnp.sum(jnp.where(lo, sq, 0.0), axis=-1, keepdims=True)
    s_hi = jnp.sum(jnp.where(lo, 0.0, sq), axis=-1, keepdims=True)
    inv = lax.rsqrt(jnp.where(lo, s_lo, s_hi) * (1.0 / DH) + EPS)
    y = x * inv * gain
    quarter = DH // 4
    return y * cos + pltpu.roll(y, LANES - quarter, 1) * sin_a + pltpu.roll(y, quarter, 1) * sin_b


def _gqa_proj_body(h_ref, g_ref, wq_ref, wk_ref, wv_ref, wvt_ref, qg_ref, kg_ref, cos_ref, sa_ref, sb_ref,
                   q_ref, k_ref, v_ref, vt_ref):
    xn = _rms(h_ref[...], g_ref[...]).astype(BF16)
    lo = lax.broadcasted_iota(jnp.int32, (TM, LANES), 1) < DH
    cos, sa, sb = cos_ref[...], sa_ref[...], sb_ref[...]
    for s in range(N_SLABS):
        sl = slice(s * LANES, (s + 1) * LANES)
        q = _head_norm_rope(_dot(xn, wq_ref[:, sl]), qg_ref[...], cos, sa, sb, lo)
        q_ref[:, sl] = (q * (DH ** -0.5 * LOG2E)).astype(BF16)
    for s in range(NKV // LANES):
        sl = slice(s * LANES, (s + 1) * LANES)
        k = _head_norm_rope(_dot(xn, wk_ref[:, sl]), kg_ref[...], cos, sa, sb, lo)
        k_ref[:, sl] = k.astype(BF16)
    v_ref[...] = _dot(xn, wv_ref[...]).astype(BF16)
    vt = _dot_nt(wvt_ref[...], xn).astype(BF16)
    ones = jnp.ones((VROWS - DH, TK), BF16)
    for c in range(TM // TK):
        for g in range(KV_HEADS):
            vt_ref[c, g * VROWS:g * VROWS + DH, :] = vt[g * DH:(g + 1) * DH, c * TK:(c + 1) * TK]
            vt_ref[c, g * VROWS + DH:(g + 1) * VROWS, :] = ones


def _gqa_proj(h, g, wq, wk, wv, wvt, qg, kg, cos, sa, sb):
    tiles_per_batch = SEQ // TM
    tab = lambda i: (jnp.where(i == N_XT, tiles_per_batch, i % tiles_per_batch), 0)
    const = lambda i: (0, 0)
    return pl.pallas_call(
        _gqa_proj_body,
        grid=(N_T,),
        in_specs=[
            pl.BlockSpec((TM, D), lambda i: (i, 0)),
            pl.BlockSpec((1, D), const),
            pl.BlockSpec((D, D), const),
            pl.BlockSpec((D, NKV), const),
            pl.BlockSpec((D, NKV), const),
            pl.BlockSpec((NKV, D), const),
            pl.BlockSpec((1, LANES), const),
            pl.BlockSpec((1, LANES), const),
            pl.BlockSpec((TM, LANES), tab),
            pl.BlockSpec((TM, LANES), tab),
            pl.BlockSpec((TM, LANES), tab),
        ],
        out_specs=[
            pl.BlockSpec((TM, D), lambda i: (i, 0)),
            pl.BlockSpec((TM, NKV), lambda i: (i, 0)),
            pl.BlockSpec((TM, NKV), lambda i: (i, 0)),
            pl.BlockSpec((TM // TK, KV_HEADS * VROWS, TK), lambda i: (i, 0, 0)),
        ],
        out_shape=[
            jax.ShapeDtypeStruct((M_PAD, D), BF16),
            jax.ShapeDtypeStruct((M_PAD, NKV), BF16),
            jax.ShapeDtypeStruct((M_PAD, NKV), BF16),
            jax.ShapeDtypeStruct((M_PAD // TK, KV_HEADS * VROWS, TK), BF16),
        ],
        compiler_params=_params(("parallel",), 48),
        name="gqa_proj",
    )(h, g, wq, wk, wv, wvt, qg, kg, cos, sa, sb)


def _rope_tables():
    t = np.arange(SEQ)
    row = (t // GRID_W).astype(np.float32)
    col = (t % GRID_W).astype(np.float32)
    sec = DH // 2
    freqs = ROPE_THETA ** (-jnp.arange(0, sec, 2, dtype=F32) / sec)
    ang = jnp.stack([row[:, None] * freqs, col[:, None] * freqs], axis=1)
    cos = jnp.cos(ang)
    sin = jnp.sin(ang)
    zero = jnp.zeros_like(sin)
    cos_h = jnp.stack([cos, cos], axis=2).reshape(SEQ, DH)
    sa_h = jnp.stack([-sin, zero], axis=2).reshape(SEQ, DH)
    sb_h = jnp.stack([zero, sin], axis=2).reshape(SEQ, DH)
    reps = LANES // DH
    pad = lambda a, v: jnp.concatenate([jnp.tile(a, (1, reps)), jnp.full((TM, LANES), v, F32)], axis=0)
    return pad(cos_h, 1.0), pad(sa_h, 0.0), pad(sb_h, 0.0)


def _gqa_body(q_ref, k_ref, km_ref, vt_ref, vtm_ref, o_ref, qh_ref, m_ref, acc_ref, *, shifted):
    lo_q = lax.broadcasted_iota(jnp.int32, (TQ, LANES), 1) < DH

    def head_slices(hd):
        s, half = divmod(hd, 2)
        j = s // 4
        g = 2 * j + half
        return slice(j * LANES, (j + 1) * LANES), slice(g * VROWS, (g + 1) * VROWS)

    for s in range(N_SLABS):
        qs = _split_heads(q_ref[:, s * LANES:(s + 1) * LANES], lo_q)
        for half in range(2):
            hd = 2 * s + half
            ksl, vsl = head_slices(hd)
            qh = qs[half * TQ:(half + 1) * TQ]
            qh_ref[hd] = qh
            st = _dot_nt(km_ref[:, ksl], qh)
            if shifted:
                m0 = jnp.max(st, axis=0, keepdims=True)
                m_ref[hd] = m0
                st = st - m0
            acc_ref[hd] = _dot(vtm_ref[vsl, :], jnp.exp2(st).astype(BF16))

    def chunk(c, carry):
        off = pl.multiple_of(c * TK, TK)

        def scores(hd):
            ksl, _ = head_slices(hd)
            return _dot_nt(k_ref[pl.ds(off, TK), ksl], qh_ref[hd])

        pending = scores(0)
        for hd in range(HEADS):
            st = pending
            if hd + 1 < HEADS:
                pending = scores(hd + 1)
            _, vsl = head_slices(hd)
            if shifted:
                m_old = m_ref[hd]
                m_new = jnp.maximum(m_old, jnp.max(st, axis=0, keepdims=True))
                m_ref[hd] = m_new
                p = jnp.exp2(st - m_new).astype(BF16)
                acc_ref[hd] = jnp.exp2(m_old - m_new) * acc_ref[hd] + _dot(vt_ref[c, vsl, :], p)
            else:
                acc_ref[hd] += _dot(vt_ref[c, vsl, :], jnp.exp2(st).astype(BF16))
        return carry

    lax.fori_loop(0, NKC, chunk, 0)

    for s in range(N_SLABS):
        a = acc_ref[2 * s]
        b = acc_ref[2 * s + 1]
        ot = jnp.concatenate([a[:DH] / a[DH:DH + 1], b[:DH] / b[DH:DH + 1]], axis=0)
        o_ref[:, s * LANES:(s + 1) * LANES] = ot.T.astype(BF16)


def _gqa_attn(q, k, k_meta, vt, vt_meta, shifted):
    return pl.pallas_call(
        functools.partial(_gqa_body, shifted=shifted),
        grid=(B, SEQ // TQ),
        in_specs=[
            pl.BlockSpec((TQ, D), lambda b, i: (b * (SEQ // TQ) + i, 0)),
            pl.BlockSpec((SEQ, NKV), lambda b, i: (b, 0)),
            pl.BlockSpec((None, N_META, NKV), lambda b, i: (b, 0, 0)),
            pl.BlockSpec((NKC, KV_HEADS * VROWS, TK), lambda b, i: (b, 0, 0)),
            pl.BlockSpec((None, KV_HEADS * VROWS, N_META), lambda b, i: (b, 0, 0)),
        ],
        out_specs=pl.BlockSpec((TQ, D), lambda b, i: (b * (SEQ // TQ) + i, 0)),
        out_shape=jax.ShapeDtypeStruct((M_PAD, D), BF16),
        scratch_shapes=[
            pltpu.VMEM((HEADS, TQ, LANES), BF16),
            pltpu.VMEM((HEADS, 1, TQ), F32),
            pltpu.VMEM((HEADS, VROWS, TQ), F32),
        ],
        compiler_params=_params(("parallel", "arbitrary"), 48),
        name="gqa_attn",
    )(q, k, k_meta, vt, vt_meta)


def _gqa_meta_body(qm_ref, k_ref, km_ref, v_ref, vm_ref, om_ref):
    lo_m = lax.broadcasted_iota(jnp.int32, (N_META, LANES), 1) < DH
    slabs_per_pair = N_SLABS // (NKV // LANES)
    for j in range(NKV // LANES):
        ksl = slice(j * LANES, (j + 1) * LANES)
        qs = jnp.concatenate(
            [_split_heads(qm_ref[:, s * LANES:(s + 1) * LANES], lo_m)
             for s in range(j * slabs_per_pair, (j + 1) * slabs_per_pair)], axis=0)
        s_m = _dot_nt(qs, km_ref[:, ksl])
        s_x = _dot_nt(qs, k_ref[:, ksl])
        m = jnp.maximum(jnp.max(s_m, axis=-1, keepdims=True), jnp.max(s_x, axis=-1, keepdims=True))
        p_m = jnp.exp2(s_m - m)
        p_x = jnp.exp2(s_x - m)
        l = jnp.sum(p_m, axis=-1, keepdims=True) + jnp.sum(p_x, axis=-1, keepdims=True)
        o = (_dot(p_m.astype(BF16), vm_ref[:, ksl]) + _dot(p_x.astype(BF16), v_ref[:, ksl])) / l
        for i in range(slabs_per_pair):
            s = j * slabs_per_pair + i
            r0 = 2 * N_META * i
            om_ref[:, s * LANES:(s + 1) * LANES] = jnp.where(
                lo_m, o[r0:r0 + N_META], o[r0 + N_META:r0 + 2 * N_META]).astype(BF16)


def _gqa_meta_attn(q, k, v):
    meta_blk = M_X // N_META
    return pl.pallas_call(
        _gqa_meta_body,
        grid=(B,),
        in_specs=[
            pl.BlockSpec((N_META, D), lambda b: (meta_blk + b, 0)),
            pl.BlockSpec((SEQ, NKV), lambda b: (b, 0)),
            pl.BlockSpec((N_META, NKV), lambda b: (meta_blk + b, 0)),
            pl.BlockSpec((SEQ, NKV), lambda b: (b, 0)),
            pl.BlockSpec((N_META, NKV), lambda b: (meta_blk + b, 0)),
        ],
        out_specs=pl.BlockSpec((N_META, D), lambda b: (b, 0)),
        out_shape=jax.ShapeDtypeStruct((M_META, D), BF16),
        compiler_params=_params(("parallel",), 48),
        name="gqa_meta_attn",
    )(q, k, k, v, v)


def _with_meta_rows(o, o_meta):
    tail = jnp.pad(o_meta, ((0, TM - M_META), (0, 0)))
    return lax.dynamic_update_slice(o, tail, (M_X, 0))


@jax.jit
def kernel(x, meta_tokens, norm_gains, ffn_w_gate, ffn_w_up, ffn_w_down, na_w_qkv, na_w_o, na_rpb, na_meta_bias,
           gqa_w_qkv, gqa_w_o, gqa_q_gain, gqa_k_gain):
    meta = jnp.broadcast_to(meta_tokens[None].astype(F32), (B, N_META, D)).reshape(M_META, D)
    h = jnp.concatenate([x.reshape(M_X, D), meta, jnp.zeros((TM - M_META, D), F32)], axis=0)
    gains = norm_gains.astype(F32).reshape(DEPTH, 6, 1, D)
    wg = ffn_w_gate.astype(BF16)
    wu = ffn_w_up.astype(BF16)
    wd = ffn_w_down.astype(BF16)
    nq = HEADS * DH

    for i in range(DEPTH):
        g = gains[i]
        j = i // 2
        h = _ffn(h, g[0], g[1], wg[i, 0], wu[i, 0], wd[i, 0], N_T)
        if i % 2 == 0:
            qkv = _na_proj(h, g[2], na_w_qkv[j].astype(BF16))
            o, o_meta = _na_attn(qkv, _na_bias_table(na_rpb[j]), _na_meta_bias_table(na_meta_bias[j]))
            w_o = na_w_o[j].astype(BF16)
        else:
            w = gqa_w_qkv[j]
            wq = w[:, :nq].reshape(D, HEADS, DH)[:, _Q_HEAD_ORDER].reshape(D, nq).astype(BF16)
            wk = w[:, nq:nq + NKV].astype(BF16)
            wv = w[:, nq + NKV:].astype(BF16)
            qg = jnp.tile(gqa_q_gain[j].astype(F32), LANES // DH).reshape(1, LANES)
            kg = jnp.tile(gqa_k_gain[j].astype(F32), LANES // DH).reshape(1, LANES)
            cos, sa, sb = _rope_tables()
            q, k, v, vt = _gqa_proj(h, g[2], wq, wk, wv, wv.T, qg, kg, cos, sa, sb)
            k_meta = k[M_X:M_X + M_META].reshape(B, N_META, NKV)
            vt_meta = vt[M_X // TK][:, :M_META].reshape(KV_HEADS * VROWS, B, N_META).transpose(1, 0, 2)
            bound = DH ** 0.5 * LOG2E * jnp.max(jnp.abs(gqa_q_gain[j])) * jnp.max(jnp.abs(gqa_k_gain[j]))
            o = lax.cond(bound <= MAX_DIRECT_EXPONENT,
                         functools.partial(_gqa_attn, shifted=False),
                         functools.partial(_gqa_attn, shifted=True),
                         q, k, k_meta, vt, vt_meta)
            o_meta = _gqa_meta_attn(q, k, v)
            w_o = gqa_w_o[j].reshape(HEADS, DH, D)[_Q_HEAD_ORDER].reshape(nq, D).astype(BF16)
        h = _oproj(_with_meta_rows(o, o_meta), w_o, h, g[3])
        n_tiles = N_XT if i == DEPTH - 1 else N_T
        h = _ffn(h, g[4], g[5], wg[i, 1], wu[i, 1], wd[i, 1], n_tiles)
    return h.reshape(B, SEQ, D)
```

```python
import functools
import math

import numpy as np
import jax
import jax.numpy as jnp
from jax import lax
from jax.experimental import pallas as pl
from jax.experimental.pallas import tpu as pltpu

D = 1024
B = 16
SEQ = 4096
DEPTH = 2
GRID_W = 64
ROWS = SEQ // GRID_W
N_META = 16
HEADS = 16
DH = 64
WIN_R = 8
WIN_C = 16
KV_HEADS = 4
NKV = KV_HEADS * DH
D_FF = 2816
EPS = 1e-6
ROPE_THETA = 10000.0

LANES = 128
N_SLABS = D // LANES
M_X = B * SEQ
M_META = B * N_META
TM = 1024
TF = 256
NF = D_FF // TF
BAND = WIN_R * GRID_W
TQ = 512
TK = 512
NKC = SEQ // TK
VROWS = DH + 16
NEG = -1e30
MAX_DIRECT_EXPONENT = 40.0
LOG2E = math.log2(math.e)

F32 = jnp.float32
BF16 = jnp.bfloat16


def _dot(a, b):
    return jnp.dot(a, b, preferred_element_type=F32)


def _dot_nt(a, b):
    return lax.dot_general(a, b, (((1,), (1,)), ((), ())), preferred_element_type=F32)


def _params(sem, vmem_mib):
    return pltpu.CompilerParams(dimension_semantics=sem, vmem_limit_bytes=vmem_mib * 2 ** 20)


def _rms(x, gain):
    ms = jnp.mean(x * x, axis=-1, keepdims=True)
    return x * lax.rsqrt(ms + EPS) * gain


def _split_heads(q128, lo):
    zero = jnp.zeros_like(q128)
    return jnp.concatenate([jnp.where(lo, q128, zero), jnp.where(lo, zero, q128)], axis=0)


def _ffn_body(h_ref, gpre_ref, gpost_ref, wg_ref, wu_ref, wd_ref, o_ref, xn_ref, acc_ref):
    xn_ref[...] = _rms(h_ref[...], gpre_ref[...]).astype(BF16)

    def gate_up(f):
        xn = xn_ref[...]
        cols = slice(f * TF, (f + 1) * TF)
        return _dot(xn, wg_ref[:, cols]), _dot(xn, wu_ref[:, cols])

    pending = gate_up(0)
    for f in range(NF):
        g, u = pending
        if f + 1 < NF:
            pending = gate_up(f + 1)
        a = (g * jax.nn.sigmoid(g) * u).astype(BF16)
        part = _dot(a, wd_ref[f * TF:(f + 1) * TF, :])
        if f == 0:
            acc_ref[...] = part
        else:
            acc_ref[...] += part
    o_ref[...] = h_ref[...] + 0.5 * _rms(acc_ref[...], gpost_ref[...])


def _ffn(h, gpre, gpost, wg, wu, wd, tm):
    const = lambda i: (0, 0)
    resident = pl.Buffered(1)
    return pl.pallas_call(
        _ffn_body,
        grid=(h.shape[0] // tm,),
        in_specs=[
            pl.BlockSpec((tm, D), lambda i: (i, 0)),
            pl.BlockSpec((1, D), const),
            pl.BlockSpec((1, D), const),
            pl.BlockSpec((D, D_FF), const, pipeline_mode=resident),
            pl.BlockSpec((D, D_FF), const, pipeline_mode=resident),
            pl.BlockSpec((D_FF, D), const, pipeline_mode=resident),
        ],
        out_specs=pl.BlockSpec((tm, D), lambda i: (i, 0)),
        out_shape=jax.ShapeDtypeStruct(h.shape, F32),
        scratch_shapes=[pltpu.VMEM((tm, D), BF16), pltpu.VMEM((tm, D), F32)],
        compiler_params=_params(("parallel",), 56),
        name="ffn",
    )(h, gpre, gpost, wg, wu, wd)


def _oproj_body(o_ref, w_ref, h_ref, g_ref, out_ref):
    m = _dot(o_ref[...], w_ref[...])
    out_ref[...] = h_ref[...] + _rms(m, g_ref[...])


def _oproj(o, w, h, g, tm):
    return pl.pallas_call(
        _oproj_body,
        grid=(h.shape[0] // tm,),
        in_specs=[
            pl.BlockSpec((tm, D), lambda i: (i, 0)),
            pl.BlockSpec((D, D), lambda i: (0, 0)),
            pl.BlockSpec((tm, D), lambda i: (i, 0)),
            pl.BlockSpec((1, D), lambda i: (0, 0)),
        ],
        out_specs=pl.BlockSpec((tm, D), lambda i: (i, 0)),
        out_shape=jax.ShapeDtypeStruct(h.shape, F32),
        compiler_params=_params(("parallel",), 48),
        name="oproj",
    )(o, w, h, g)


def _na_proj_body(h_ref, g_ref, w_ref, o_ref, xn_ref):
    n = pl.program_id(1)

    @pl.when(n == 0)
    def _():
        xn_ref[...] = _rms(h_ref[...], g_ref[...]).astype(BF16)

    scale = jnp.where(n == 0, DH ** -0.5, 1.0).astype(F32)
    o_ref[...] = (_dot(xn_ref[...], w_ref[...]) * scale).astype(BF16)


def _na_proj(h, g, w, tm):
    return pl.pallas_call(
        _na_proj_body,
        grid=(h.shape[0] // tm, 3),
        in_specs=[
            pl.BlockSpec((tm, D), lambda i, n: (i, 0)),
            pl.BlockSpec((1, D), lambda i, n: (0, 0)),
            pl.BlockSpec((D, D), lambda i, n: (0, n)),
        ],
        out_specs=pl.BlockSpec((tm, D), lambda i, n: (i, n)),
        out_shape=jax.ShapeDtypeStruct((h.shape[0], 3 * D), BF16),
        scratch_shapes=[pltpu.VMEM((tm, D), BF16)],
        compiler_params=_params(("parallel", "arbitrary"), 48),
        name="na_proj",
    )(h, g, w)


def _softmax_pv(scores, values):
    m = None
    for s in scores:
        mi = jnp.max(s, axis=-1, keepdims=True)
        m = mi if m is None else jnp.maximum(m, mi)
    l = None
    o = None
    for s, v in zip(scores, values):
        p = jnp.exp(s - m)
        li = jnp.sum(p, axis=-1, keepdims=True)
        oi = _dot(p.astype(BF16), v)
        l = li if l is None else l + li
        o = oi if o is None else o + oi
    return o / l


def _na_body(q_ref, k_ref, v_ref, qm_ref, km_ref, vm_ref, bias_ref, mb_ref, o_ref, om_ref):
    r = pl.program_id(1)
    rs = jnp.clip(r - WIN_R // 2, 0, ROWS - WIN_R)
    start = pl.multiple_of(rs * GRID_W, GRID_W)
    lo = lax.broadcasted_iota(jnp.int32, (GRID_W, LANES), 1) < DH

    def scores(s):
        sl = slice(s * LANES, (s + 1) * LANES)
        qs = _split_heads(q_ref[:, sl], lo)
        sc = _dot_nt(qs, k_ref[pl.ds(start, BAND), sl]) + bias_ref[2 * s:2 * s + 2].reshape(2 * GRID_W, BAND)
        sm = _dot_nt(qs, km_ref[:, sl]) + mb_ref[s]
        return sm, sc

    pending = scores(0)
    for s in range(N_SLABS):
        sl = slice(s * LANES, (s + 1) * LANES)
        sm, sc = pending
        if s + 1 < N_SLABS:
            pending = scores(s + 1)
        o = _softmax_pv([sm, sc], [vm_ref[:, sl], v_ref[pl.ds(start, BAND), sl]])
        o_ref[:, sl] = jnp.where(lo, o[:GRID_W], o[GRID_W:]).astype(BF16)

    @pl.when(r == 0)
    def _():
        lo_m = lax.broadcasted_iota(jnp.int32, (N_META, LANES), 1) < DH
        for s in range(N_SLABS):
            sl = slice(s * LANES, (s + 1) * LANES)
            qs = _split_heads(qm_ref[:, sl], lo_m)
            o = _softmax_pv([_dot_nt(qs, km_ref[:, sl])], [vm_ref[:, sl]])
            om_ref[:, sl] = jnp.where(lo_m, o[:N_META], o[N_META:]).astype(BF16)


def _na_bias_row(r):
    rs = jnp.clip(r - WIN_R // 2, 0, ROWS - WIN_R)
    return rs - r + (WIN_R - 1)


def _na_attn(qkv_x, qkv_m, bias, mbias):
    return pl.pallas_call(
        _na_body,
        grid=(B, ROWS),
        in_specs=[
            pl.BlockSpec((GRID_W, D), lambda b, r: (b * ROWS + r, 0)),
            pl.BlockSpec((SEQ, D), lambda b, r: (b, 1)),
            pl.BlockSpec((SEQ, D), lambda b, r: (b, 2)),
            pl.BlockSpec((N_META, D), lambda b, r: (b, 0)),
            pl.BlockSpec((N_META, D), lambda b, r: (b, 1)),
            pl.BlockSpec((N_META, D), lambda b, r: (b, 2)),
            pl.BlockSpec((HEADS, None, GRID_W, BAND), lambda b, r: (0, _na_bias_row(r), 0, 0)),
            pl.BlockSpec((N_SLABS, 2 * GRID_W, N_META), lambda b, r: (0, 0, 0)),
        ],
        out_specs=[
            pl.BlockSpec((GRID_W, D), lambda b, r: (b * ROWS + r, 0)),
            pl.BlockSpec((N_META, D), lambda b, r: (b, 0)),
        ],
        out_shape=[
            jax.ShapeDtypeStruct((M_X, D), BF16),
            jax.ShapeDtypeStruct((M_META, D), BF16),
        ],
        compiler_params=_params(("parallel", "arbitrary"), 56),
        name="na_attn",
    )(qkv_x, qkv_x, qkv_x, qkv_m, qkv_m, qkv_m, bias, mbias)


def _na_bias_table(rpb):
    cols = np.arange(GRID_W)
    col_start = np.clip(cols - WIN_C // 2, 0, GRID_W - WIN_C)
    kc = np.arange(GRID_W)
    valid = (kc[None, :] >= col_start[:, None]) & (kc[None, :] < col_start[:, None] + WIN_C)
    col_rel = kc[None, :] - cols[:, None] + (WIN_C - 1)
    onehot = np.zeros((GRID_W, GRID_W, 2 * WIN_C - 1), np.float32)
    ci, ki = np.nonzero(valid)
    onehot[ci, ki, col_rel[ci, ki]] = 1.0
    base = jnp.einsum("hrx,ckx->hrck", rpb.astype(F32), onehot, precision=lax.Precision.HIGHEST)
    base = jnp.where(valid[None, None], base, NEG)
    t = jnp.stack([base[:, d:d + WIN_R] for d in range(WIN_R)], axis=1)
    return jnp.transpose(t, (0, 1, 3, 2, 4)).reshape(HEADS, WIN_R, GRID_W, BAND)


def _na_meta_bias_table(meta_bias):
    mb = meta_bias.astype(F32).reshape(N_SLABS, 2, 1, N_META)
    return jnp.broadcast_to(mb, (N_SLABS, 2, GRID_W, N_META)).reshape(N_SLABS, 2 * GRID_W, N_META)


_Q_HEAD_ORDER = np.array([8 * j + 4 * hi + i for j in range(2) for i in range(4) for hi in range(2)])


def _head_norm_rope(x, gain, cos, sin_a, sin_b, lo):
    sq = x * x
    s_lo = jnp.sum(jnp.where(lo, sq, 0.0), axis=-1, keepdims=True)
    s_hi = jnp.sum(jnp.where(lo, 0.0, sq), axis=-1, keepdims=True)
    inv = lax.rsqrt(jnp.where(lo, s_lo, s_hi) * (1.0 / DH) + EPS)
    y = x * inv * gain
    quarter = DH // 4
    return y * cos + pltpu.roll(y, LANES - quarter, 1) * sin_a + pltpu.roll(y, quarter, 1) * sin_b


def _gqa_proj_body(h_ref, g_ref, wq_ref, wk_ref, wv_ref, wvt_ref, qg_ref, kg_ref, cos_ref, sa_ref, sb_ref,
                   q_ref, k_ref, v_ref, vt_ref):
    tm = h_ref.shape[0]
    tkc = vt_ref.shape[2]
    xn = _rms(h_ref[...], g_ref[...]).astype(BF16)
    qf = _dot(xn, wq_ref[...])
    kf = _dot(xn, wk_ref[...])
    v_ref[...] = _dot(xn, wv_ref[...]).astype(BF16)
    vt = _dot_nt(wvt_ref[...], xn).astype(BF16)
    ones = jnp.ones((VROWS - DH, tkc), BF16)
    for c in range(tm // tkc):
        for g in range(KV_HEADS):
            vt_ref[c, g * VROWS:g * VROWS + DH, :] = vt[g * DH:(g + 1) * DH, c * tkc:(c + 1) * tkc]
            vt_ref[c, g * VROWS + DH:(g + 1) * VROWS, :] = ones
    lo = lax.broadcasted_iota(jnp.int32, (tm, LANES), 1) < DH
    cos, sa, sb = cos_ref[...], sa_ref[...], sb_ref[...]
    for s in range(N_SLABS):
        sl = slice(s * LANES, (s + 1) * LANES)
        q = _head_norm_rope(qf[:, sl], qg_ref[...], cos, sa, sb, lo)
        q_ref[:, sl] = (q * (DH ** -0.5 * LOG2E)).astype(BF16)
    for s in range(NKV // LANES):
        sl = slice(s * LANES, (s + 1) * LANES)
        k_ref[:, sl] = _head_norm_rope(kf[:, sl], kg_ref[...], cos, sa, sb, lo).astype(BF16)


def _gqa_proj(h, g, wq, wk, wv, wvt, qg, kg, tables, tm):
    m = h.shape[0]
    tkc = min(TK, tm)
    n_tab = tables[0].shape[0] // tm
    tab = lambda i: (i % n_tab, 0)
    const = lambda i: (0, 0)
    return pl.pallas_call(
        _gqa_proj_body,
        grid=(m // tm,),
        in_specs=[
            pl.BlockSpec((tm, D), lambda i: (i, 0)),
            pl.BlockSpec((1, D), const),
            pl.BlockSpec((D, D), const),
            pl.BlockSpec((D, NKV), const),
            pl.BlockSpec((D, NKV), const),
            pl.BlockSpec((NKV, D), const),
            pl.BlockSpec((1, LANES), const),
            pl.BlockSpec((1, LANES), const),
            pl.BlockSpec((tm, LANES), tab),
            pl.BlockSpec((tm, LANES), tab),
            pl.BlockSpec((tm, LANES), tab),
        ],
        out_specs=[
            pl.BlockSpec((tm, D), lambda i: (i, 0)),
            pl.BlockSpec((tm, NKV), lambda i: (i, 0)),
            pl.BlockSpec((tm, NKV), lambda i: (i, 0)),
            pl.BlockSpec((tm // tkc, KV_HEADS * VROWS, tkc), lambda i: (i, 0, 0)),
        ],
        out_shape=[
            jax.ShapeDtypeStruct((m, D), BF16),
            jax.ShapeDtypeStruct((m, NKV), BF16),
            jax.ShapeDtypeStruct((m, NKV), BF16),
            jax.ShapeDtypeStruct((m // tkc, KV_HEADS * VROWS, tkc), BF16),
        ],
        compiler_params=_params(("parallel",), 48),
        name="gqa_proj",
    )(h, g, wq, wk, wv, wvt, qg, kg, *tables)


def _rope_tables():
    t = np.arange(SEQ)
    row = (t // GRID_W).astype(np.float32)
    col = (t % GRID_W).astype(np.float32)
    sec = DH // 2
    freqs = ROPE_THETA ** (-jnp.arange(0, sec, 2, dtype=F32) / sec)
    ang = jnp.stack([row[:, None] * freqs, col[:, None] * freqs], axis=1)
    cos = jnp.cos(ang)
    sin = jnp.sin(ang)
    zero = jnp.zeros_like(sin)
    cos_h = jnp.stack([cos, cos], axis=2).reshape(SEQ, DH)
    sa_h = jnp.stack([-sin, zero], axis=2).reshape(SEQ, DH)
    sb_h = jnp.stack([zero, sin], axis=2).reshape(SEQ, DH)
    reps = LANES // DH
    return tuple(jnp.tile(a, (1, reps)) for a in (cos_h, sa_h, sb_h))


def _identity_rope_tables(m):
    return (jnp.ones((m, LANES), F32), jnp.zeros((m, LANES), F32), jnp.zeros((m, LANES), F32))


def _gqa_body(q_ref, k_ref, vt_ref, vtm_ref, o_ref, qt_ref, m_ref, acc_ref, *, shifted):
    row_lo = lax.broadcasted_iota(jnp.int32, (LANES, TQ), 0) < DH

    def head_slices(hd):
        s, half = divmod(hd, 2)
        j = s // 4
        g = 2 * j + half
        return slice(j * LANES, (j + 1) * LANES), slice(g * VROWS, (g + 1) * VROWS)

    for s in range(N_SLABS):
        qt = q_ref[:, s * LANES:(s + 1) * LANES].astype(F32).T
        zero = jnp.zeros_like(qt)
        qt_ref[2 * s] = jnp.where(row_lo, qt, zero).astype(BF16)
        qt_ref[2 * s + 1] = jnp.where(row_lo, zero, qt).astype(BF16)

    def chunk(c, first, last):
        rows = TK + N_META if last else TK
        off = c * TK if isinstance(c, int) else pl.multiple_of(c * TK, TK)

        def scores(hd):
            ksl, _ = head_slices(hd)
            return _dot(k_ref[pl.ds(off, rows), ksl], qt_ref[hd])

        pending = scores(0)
        for hd in range(HEADS):
            st = pending
            if hd + 1 < HEADS:
                pending = scores(hd + 1)
            _, vsl = head_slices(hd)
            if shifted:
                m_new = jnp.max(st, axis=0, keepdims=True)
                if not first:
                    m_old = m_ref[hd]
                    m_new = jnp.maximum(m_old, m_new)
                m_ref[hd] = m_new
                st = st - m_new
            p = jnp.exp2(st).astype(BF16)
            pv = _dot(vt_ref[c, vsl, :], p[:TK])
            if last:
                pv = pv + _dot(vtm_ref[vsl, :], p[TK:])
            if first:
                acc_ref[hd] = pv
            elif shifted:
                acc_ref[hd] = jnp.exp2(m_old - m_new) * acc_ref[hd] + pv
            else:
                acc_ref[hd] += pv

    def middle(c, carry):
        chunk(c, False, False)
        return carry

    chunk(0, True, False)
    lax.fori_loop(1, NKC - 1, middle, 0)
    chunk(NKC - 1, False, True)

    for s in range(N_SLABS):
        a = acc_ref[2 * s]
        b = acc_ref[2 * s + 1]
        ot = jnp.concatenate([a[:DH] / a[DH:DH + 1], b[:DH] / b[DH:DH + 1]], axis=0)
        o_ref[:, s * LANES:(s + 1) * LANES] = ot.T.astype(BF16)


def _gqa_attn(q, k_all, vt, vt_meta, shifted):
    return pl.pallas_call(
        functools.partial(_gqa_body, shifted=shifted),
        grid=(B, SEQ // TQ),
        in_specs=[
            pl.BlockSpec((TQ, D), lambda b, i: (b * (SEQ // TQ) + i, 0)),
            pl.BlockSpec((None, SEQ + N_META, NKV), lambda b, i: (b, 0, 0)),
            pl.BlockSpec((NKC, KV_HEADS * VROWS, TK), lambda b, i: (b, 0, 0)),
            pl.BlockSpec((None, KV_HEADS * VROWS, N_META), lambda b, i: (b, 0, 0)),
        ],
        out_specs=pl.BlockSpec((TQ, D), lambda b, i: (b * (SEQ // TQ) + i, 0)),
        out_shape=jax.ShapeDtypeStruct((M_X, D), BF16),
        scratch_shapes=[
            pltpu.VMEM((HEADS, LANES, TQ), BF16),
            pltpu.VMEM((HEADS, 1, TQ), F32),
            pltpu.VMEM((HEADS, VROWS, TQ), F32),
        ],
        compiler_params=_params(("parallel", "arbitrary"), 48),
        name="gqa_attn",
    )(q, k_all, vt, vt_meta)


def _gqa_meta_body(qm_ref, k_ref, km_ref, v_ref, vm_ref, om_ref):
    lo_m = lax.broadcasted_iota(jnp.int32, (N_META, LANES), 1) < DH
    slabs_per_pair = N_SLABS // (NKV // LANES)
    for j in range(NKV // LANES):
        ksl = slice(j * LANES, (j + 1) * LANES)
        qs = jnp.concatenate(
            [_split_heads(qm_ref[:, s * LANES:(s + 1) * LANES], lo_m)
             for s in range(j * slabs_per_pair, (j + 1) * slabs_per_pair)], axis=0)
        s_m = _dot_nt(qs, km_ref[:, ksl])
        s_x = _dot_nt(qs, k_ref[:, ksl])
        m = jnp.maximum(jnp.max(s_m, axis=-1, keepdims=True), jnp.max(s_x, axis=-1, keepdims=True))
        p_m = jnp.exp2(s_m - m)
        p_x = jnp.exp2(s_x - m)
        l = jnp.sum(p_m, axis=-1, keepdims=True) + jnp.sum(p_x, axis=-1, keepdims=True)
        o = (_dot(p_m.astype(BF16), vm_ref[:, ksl]) + _dot(p_x.astype(BF16), v_ref[:, ksl])) / l
        for i in range(slabs_per_pair):
            s = j * slabs_per_pair + i
            r0 = 2 * N_META * i
            om_ref[:, s * LANES:(s + 1) * LANES] = jnp.where(
                lo_m, o[r0:r0 + N_META], o[r0 + N_META:r0 + 2 * N_META]).astype(BF16)


def _gqa_meta_attn(q_m, k_x, k_m, v_x, v_m):
    return pl.pallas_call(
        _gqa_meta_body,
        grid=(B,),
        in_specs=[
            pl.BlockSpec((N_META, D), lambda b: (b, 0)),
            pl.BlockSpec((SEQ, NKV), lambda b: (b, 0)),
            pl.BlockSpec((N_META, NKV), lambda b: (b, 0)),
            pl.BlockSpec((SEQ, NKV), lambda b: (b, 0)),
            pl.BlockSpec((N_META, NKV), lambda b: (b, 0)),
        ],
        out_specs=pl.BlockSpec((N_META, D), lambda b: (b, 0)),
        out_shape=jax.ShapeDtypeStruct((M_META, D), BF16),
        compiler_params=_params(("parallel",), 48),
        name="gqa_meta_attn",
    )(q_m, k_x, k_m, v_x, v_m)


@jax.jit
def kernel(x, meta_tokens, norm_gains, ffn_w_gate, ffn_w_up, ffn_w_down, na_w_qkv, na_w_o, na_rpb, na_meta_bias,
           gqa_w_qkv, gqa_w_o, gqa_q_gain, gqa_k_gain):
    hx = x.astype(F32).reshape(M_X, D)
    hm = jnp.broadcast_to(meta_tokens[None].astype(F32), (B, N_META, D)).reshape(M_META, D)
    streams = [(hx, TM), (hm, M_META)]
    gains = norm_gains.astype(F32).reshape(DEPTH, 6, 1, D)
    wg = ffn_w_gate.astype(BF16)
    wu = ffn_w_up.astype(BF16)
    wd = ffn_w_down.astype(BF16)
    nq = HEADS * DH

    for i in range(DEPTH):
        g = gains[i]
        j = i // 2
        streams = [(_ffn(h, g[0], g[1], wg[i, 0], wu[i, 0], wd[i, 0], tm), tm) for h, tm in streams]
        if i % 2 == 0:
            w = na_w_qkv[j].astype(BF16)
            qkv_x, qkv_m = [_na_proj(h, g[2], w, tm) for h, tm in streams]
            outs = _na_attn(qkv_x, qkv_m, _na_bias_table(na_rpb[j]), _na_meta_bias_table(na_meta_bias[j]))
            w_o = na_w_o[j].astype(BF16)
        else:
            w = gqa_w_qkv[j]
            wq = w[:, :nq].reshape(D, HEADS, DH)[:, _Q_HEAD_ORDER].reshape(D, nq).astype(BF16)
            wk = w[:, nq:nq + NKV].astype(BF16)
            wv = w[:, nq + NKV:].astype(BF16)
            qg = jnp.tile(gqa_q_gain[j].astype(F32), LANES // DH).reshape(1, LANES)
            kg = jnp.tile(gqa_k_gain[j].astype(F32), LANES // DH).reshape(1, LANES)
            tables = [_rope_tables(), _identity_rope_tables(M_META)]
            (q_x, k_x, v_x, vt_x), (q_m, k_m, v_m, vt_m) = [
                _gqa_proj(h, g[2], wq, wk, wv, wv.T, qg, kg, tab, tm) for (h, tm), tab in zip(streams, tables)]
            k_all = jnp.concatenate([k_x.reshape(B, SEQ, NKV), k_m.reshape(B, N_META, NKV)], axis=1)
            vt_meta = vt_m[0].reshape(KV_HEADS * VROWS, B, N_META).transpose(1, 0, 2)
            bound = DH ** 0.5 * LOG2E * jnp.max(jnp.abs(gqa_q_gain[j])) * jnp.max(jnp.abs(gqa_k_gain[j]))
            o_x = lax.cond(bound <= MAX_DIRECT_EXPONENT,
                           functools.partial(_gqa_attn, shifted=False),
                           functools.partial(_gqa_attn, shifted=True),
                           q_x, k_all, vt_x, vt_meta)
            outs = (o_x, _gqa_meta_attn(q_m, k_x, k_m, v_x, v_m))
            w_o = gqa_w_o[j].reshape(HEADS, DH, D)[_Q_HEAD_ORDER].reshape(nq, D).astype(BF16)
        streams = [(_oproj(o, w_o, h, g[3], tm), tm) for o, (h, tm) in zip(outs, streams)]
        if i == DEPTH - 1:
            streams = streams[:1]
        streams = [(_ffn(h, g[4], g[5], wg[i, 1], wu[i, 1], wd[i, 1], tm), tm) for h, tm in streams]
    return streams[0][0].reshape(B, SEQ, D)
```

```python
import functools
import math

import numpy as np
import jax
import jax.numpy as jnp
from jax import lax
from jax.experimental import pallas as pl
from jax.experimental.pallas import tpu as pltpu

D = 1024
B = 16
SEQ = 4096
DEPTH = 2
GRID_W = 64
ROWS = SEQ // GRID_W
N_META = 16
HEADS = 16
DH = 64
WIN_R = 8
WIN_C = 16
KV_HEADS = 4
NKV = KV_HEADS * DH
D_FF = 2816
EPS = 1e-6
ROPE_THETA = 10000.0

LANES = 128
N_SLABS = D // LANES
M_X = B * SEQ
M_META = B * N_META
TM = 1024
FFN_SUB = 2
OPROJ_SUB = 4
TF = 256
NF = D_FF // TF
BAND = WIN_R * GRID_W
TQ = 512
TK = 512
NKC = SEQ // TK
VROWS = DH + 16
NEG = -1e30
MAX_DIRECT_EXPONENT = 40.0
LOG2E = math.log2(math.e)

F32 = jnp.float32
BF16 = jnp.bfloat16


def _dot(a, b):
    return jnp.dot(a, b, preferred_element_type=F32)


def _dot_nt(a, b):
    return lax.dot_general(a, b, (((1,), (1,)), ((), ())), preferred_element_type=F32)


def _params(sem, vmem_mib):
    return pltpu.CompilerParams(dimension_semantics=sem, vmem_limit_bytes=vmem_mib * 2 ** 20)


def _rms(x, gain):
    ms = jnp.mean(x * x, axis=-1, keepdims=True)
    return x * lax.rsqrt(ms + EPS) * gain


def _split_heads(q128, lo):
    zero = jnp.zeros_like(q128)
    return jnp.concatenate([jnp.where(lo, q128, zero), jnp.where(lo, zero, q128)], axis=0)


def _ffn_body(h_ref, gpre_ref, gpost_ref, wg_ref, wu_ref, wd_ref, o_ref, xn_ref, acc_ref):
    tm = h_ref.shape[0]
    n_sub = FFN_SUB if tm % (FFN_SUB * LANES) == 0 else 1
    sub = tm // n_sub

    def rows(r):
        return slice(r * sub, (r + 1) * sub)

    def prologue(r):
        xn_ref[rows(r)] = _rms(h_ref[rows(r)], gpre_ref[...]).astype(BF16)

    def gate_up(r, f):
        xn = xn_ref[rows(r)]
        cols = slice(f * TF, (f + 1) * TF)
        return _dot(xn, wg_ref[:, cols]), _dot(xn, wu_ref[:, cols])

    def epilogue(r):
        o_ref[rows(r)] = h_ref[rows(r)] + 0.5 * _rms(acc_ref[rows(r)], gpost_ref[...])

    units = [(r, f) for r in range(n_sub) for f in range(NF)]
    prologue(0)
    pending = gate_up(*units[0])
    for idx, (r, f) in enumerate(units):
        g, u = pending
        if f == 0 and r + 1 < n_sub:
            prologue(r + 1)
        if idx + 1 < len(units):
            pending = gate_up(*units[idx + 1])
        a = (g * jax.nn.sigmoid(g) * u).astype(BF16)
        part = _dot(a, wd_ref[f * TF:(f + 1) * TF, :])
        if f == 0:
            acc_ref[rows(r)] = part
        else:
            acc_ref[rows(r)] += part
        if f == NF - 1:
            epilogue(r)


def _ffn(h, gpre, gpost, wg, wu, wd, tm):
    const = lambda i: (0, 0)
    resident = pl.Buffered(1)
    return pl.pallas_call(
        _ffn_body,
        grid=(h.shape[0] // tm,),
        in_specs=[
            pl.BlockSpec((tm, D), lambda i: (i, 0)),
            pl.BlockSpec((1, D), const),
            pl.BlockSpec((1, D), const),
            pl.BlockSpec((D, D_FF), const, pipeline_mode=resident),
            pl.BlockSpec((D, D_FF), const, pipeline_mode=resident),
            pl.BlockSpec((D_FF, D), const, pipeline_mode=resident),
        ],
        out_specs=pl.BlockSpec((tm, D), lambda i: (i, 0)),
        out_shape=jax.ShapeDtypeStruct(h.shape, F32),
        scratch_shapes=[pltpu.VMEM((tm, D), BF16), pltpu.VMEM((tm, D), F32)],
        compiler_params=_params(("parallel",), 56),
        name="ffn",
    )(h, gpre, gpost, wg, wu, wd)


def _oproj_body(o_ref, w_ref, h_ref, g_ref, out_ref):
    tm = o_ref.shape[0]
    n_sub = OPROJ_SUB if tm % (OPROJ_SUB * LANES) == 0 else 1
    sub = tm // n_sub

    def project(r):
        return _dot(o_ref[r * sub:(r + 1) * sub], w_ref[...])

    pending = project(0)
    for r in range(n_sub):
        m = pending
        if r + 1 < n_sub:
            pending = project(r + 1)
        rows = slice(r * sub, (r + 1) * sub)
        out_ref[rows] = h_ref[rows] + _rms(m, g_ref[...])


def _oproj(o, w, h, g, tm):
    return pl.pallas_call(
        _oproj_body,
        grid=(h.shape[0] // tm,),
        in_specs=[
            pl.BlockSpec((tm, D), lambda i: (i, 0)),
            pl.BlockSpec((D, D), lambda i: (0, 0)),
            pl.BlockSpec((tm, D), lambda i: (i, 0)),
            pl.BlockSpec((1, D), lambda i: (0, 0)),
        ],
        out_specs=pl.BlockSpec((tm, D), lambda i: (i, 0)),
        out_shape=jax.ShapeDtypeStruct(h.shape, F32),
        compiler_params=_params(("parallel",), 48),
        name="oproj",
    )(o, w, h, g)


def _na_proj_body(h_ref, g_ref, w_ref, o_ref):
    xn = _rms(h_ref[...], g_ref[...]).astype(BF16)
    scales = (DH ** -0.5 * LOG2E, 1.0, 1.0)
    for n, scale in enumerate(scales):
        cols = slice(n * D, (n + 1) * D)
        o_ref[:, cols] = (_dot(xn, w_ref[:, cols]) * scale).astype(BF16)


def _na_proj(h, g, w, tm):
    return pl.pallas_call(
        _na_proj_body,
        grid=(h.shape[0] // tm,),
        in_specs=[
            pl.BlockSpec((tm, D), lambda i: (i, 0)),
            pl.BlockSpec((1, D), lambda i: (0, 0)),
            pl.BlockSpec((D, 3 * D), lambda i: (0, 0), pipeline_mode=pl.Buffered(1)),
        ],
        out_specs=pl.BlockSpec((tm, 3 * D), lambda i: (i, 0)),
        out_shape=jax.ShapeDtypeStruct((h.shape[0], 3 * D), BF16),
        compiler_params=_params(("parallel",), 48),
        name="na_proj",
    )(h, g, w)


NA_RB = 4


def _na_body(q_ref, k_ref, v_ref, qm_ref, km_ref, vm_ref, *refs):
    bias_refs = refs[:NA_RB]
    mb_ref, o_ref, om_ref = refs[NA_RB:]
    rb = pl.program_id(1)
    lo = lax.broadcasted_iota(jnp.int32, (GRID_W, LANES), 1) < DH
    ones = jnp.ones((BAND, LANES), BF16)
    ones_m = jnp.ones((N_META, LANES), BF16)

    def band_start(rr):
        rs = jnp.clip(rb * NA_RB + rr - WIN_R // 2, 0, ROWS - WIN_R)
        return pl.multiple_of(rs * GRID_W, GRID_W)

    def scores(t):
        rr, s = divmod(t, N_SLABS)
        sl = slice(s * LANES, (s + 1) * LANES)
        qs = _split_heads(q_ref[rr * GRID_W:(rr + 1) * GRID_W, sl], lo)
        sc = _dot_nt(qs, k_ref[pl.ds(band_start(rr), BAND), sl]) \
            + bias_refs[rr][2 * s:2 * s + 2].reshape(2 * GRID_W, BAND)
        sm = _dot_nt(qs, km_ref[:, sl]) + mb_ref[s]
        return sm, sc

    def finish(t, sm, sc):
        rr, s = divmod(t, N_SLABS)
        sl = slice(s * LANES, (s + 1) * LANES)
        m = jnp.maximum(jnp.max(sm, axis=-1, keepdims=True), jnp.max(sc, axis=-1, keepdims=True))
        vb = jnp.concatenate([v_ref[pl.ds(band_start(rr), BAND), sl], ones], axis=1)
        vm = jnp.concatenate([vm_ref[:, sl], ones_m], axis=1)
        ol = _dot(jnp.exp2(sc - m).astype(BF16), vb) + _dot(jnp.exp2(sm - m).astype(BF16), vm)
        o = ol[:, :LANES] / ol[:, LANES:]
        o_ref[rr * GRID_W:(rr + 1) * GRID_W, sl] = jnp.where(lo, o[:GRID_W], o[GRID_W:]).astype(BF16)

    n = NA_RB * N_SLABS
    pending = scores(0)
    for t in range(n):
        sm, sc = pending
        if t + 1 < n:
            pending = scores(t + 1)
        finish(t, sm, sc)

    @pl.when(rb == 0)
    def _():
        lo_m = lax.broadcasted_iota(jnp.int32, (N_META, LANES), 1) < DH
        for s in range(N_SLABS):
            sl = slice(s * LANES, (s + 1) * LANES)
            qs = _split_heads(qm_ref[:, sl], lo_m)
            st = _dot_nt(qs, km_ref[:, sl])
            p = jnp.exp2(st - jnp.max(st, axis=-1, keepdims=True))
            o = _dot(p.astype(BF16), vm_ref[:, sl]) / jnp.sum(p, axis=-1, keepdims=True)
            om_ref[:, sl] = jnp.where(lo_m, o[:N_META], o[N_META:]).astype(BF16)


def _na_bias_row(r):
    rs = jnp.clip(r - WIN_R // 2, 0, ROWS - WIN_R)
    return rs - r + (WIN_R - 1)


def _na_attn(qkv_x, qkv_m, bias, mbias):
    nrb = ROWS // NA_RB
    once = pl.Buffered(1)
    bias_specs = [
        pl.BlockSpec((HEADS, None, GRID_W, BAND), lambda b, r, rr=rr: (0, _na_bias_row(NA_RB * r + rr), 0, 0))
        for rr in range(NA_RB)]
    return pl.pallas_call(
        _na_body,
        grid=(B, nrb),
        in_specs=[
            pl.BlockSpec((NA_RB * GRID_W, D), lambda b, r: (b * nrb + r, 0)),
            pl.BlockSpec((SEQ, D), lambda b, r: (b, 1), pipeline_mode=once),
            pl.BlockSpec((SEQ, D), lambda b, r: (b, 2), pipeline_mode=once),
            pl.BlockSpec((N_META, D), lambda b, r: (b, 0)),
            pl.BlockSpec((N_META, D), lambda b, r: (b, 1)),
            pl.BlockSpec((N_META, D), lambda b, r: (b, 2)),
            *bias_specs,
            pl.BlockSpec((N_SLABS, 2 * GRID_W, N_META), lambda b, r: (0, 0, 0)),
        ],
        out_specs=[
            pl.BlockSpec((NA_RB * GRID_W, D), lambda b, r: (b * nrb + r, 0)),
            pl.BlockSpec((N_META, D), lambda b, r: (b, 0)),
        ],
        out_shape=[
            jax.ShapeDtypeStruct((M_X, D), BF16),
            jax.ShapeDtypeStruct((M_META, D), BF16),
        ],
        compiler_params=_params(("parallel", "arbitrary"), 56),
        name="na_attn",
    )(qkv_x, qkv_x, qkv_x, qkv_m, qkv_m, qkv_m, *([bias] * NA_RB), mbias)


def _na_bias_table(rpb):
    cols = np.arange(GRID_W)
    col_start = np.clip(cols - WIN_C // 2, 0, GRID_W - WIN_C)
    kc = np.arange(GRID_W)
    valid = (kc[None, :] >= col_start[:, None]) & (kc[None, :] < col_start[:, None] + WIN_C)
    col_rel = kc[None, :] - cols[:, None] + (WIN_C - 1)
    onehot = np.zeros((GRID_W, GRID_W, 2 * WIN_C - 1), np.float32)
    ci, ki = np.nonzero(valid)
    onehot[ci, ki, col_rel[ci, ki]] = 1.0
    base = jnp.einsum("hrx,ckx->hrck", rpb.astype(F32), onehot, precision=lax.Precision.HIGHEST)
    base = jnp.where(valid[None, None], base, NEG)
    t = jnp.stack([base[:, d:d + WIN_R] for d in range(WIN_R)], axis=1)
    return jnp.transpose(t, (0, 1, 3, 2, 4)).reshape(HEADS, WIN_R, GRID_W, BAND)


def _na_meta_bias_table(meta_bias):
    mb = meta_bias.astype(F32).reshape(N_SLABS, 2, 1, N_META)
    return jnp.broadcast_to(mb, (N_SLABS, 2, GRID_W, N_META)).reshape(N_SLABS, 2 * GRID_W, N_META)


_Q_HEAD_ORDER = np.array([8 * j + 4 * hi + i for j in range(2) for i in range(4) for hi in range(2)])


def _head_norm_rope(x, gain, cos, sin_a, sin_b, lo):
    sq = x * x
    s_lo = jnp.sum(jnp.where(lo, sq, 0.0), axis=-1, keepdims=True)
    s_hi = jnp.sum(jnp.where(lo, 0.0, sq), axis=-1, keepdims=True)
    inv = lax.rsqrt(jnp.where(lo, s_lo, s_hi) * (1.0 / DH) + EPS)
    y = x * inv * gain
    quarter = DH // 4
    return y * cos + pltpu.roll(y, LANES - quarter, 1) * sin_a + pltpu.roll(y, quarter, 1) * sin_b


def _gqa_proj_body(h_ref, g_ref, wq_ref, wk_ref, wv_ref, wvt_ref, qg_ref, kg_ref, cos_ref, sa_ref, sb_ref,
                   q_ref, k_ref, v_ref, vt_ref):
    tm = h_ref.shape[0]
    tkc = vt_ref.shape[2]
    xn = _rms(h_ref[...], g_ref[...]).astype(BF16)
    qf = _dot(xn, wq_ref[...])
    kf = _dot(xn, wk_ref[...])
    v_ref[...] = _dot(xn, wv_ref[...]).astype(BF16)
    vt = _dot_nt(wvt_ref[...], xn).astype(BF16)
    ones = jnp.ones((VROWS - DH, tkc), BF16)
    for c in range(tm // tkc):
        for g in range(KV_HEADS):
            vt_ref[c, g * VROWS:g * VROWS + DH, :] = vt[g * DH:(g + 1) * DH, c * tkc:(c + 1) * tkc]
            vt_ref[c, g * VROWS + DH:(g + 1) * VROWS, :] = ones
    lo = lax.broadcasted_iota(jnp.int32, (tm, LANES), 1) < DH
    cos, sa, sb = cos_ref[...], sa_ref[...], sb_ref[...]
    for s in range(N_SLABS):
        sl = slice(s * LANES, (s + 1) * LANES)
        q = _head_norm_rope(qf[:, sl], qg_ref[...], cos, sa, sb, lo)
        q_ref[:, sl] = (q * (DH ** -0.5 * LOG2E)).astype(BF16)
    for s in range(NKV // LANES):
        sl = slice(s * LANES, (s + 1) * LANES)
        k_ref[:, sl] = _head_norm_rope(kf[:, sl], kg_ref[...], cos, sa, sb, lo).astype(BF16)


def _gqa_proj(h, g, wq, wk, wv, wvt, qg, kg, tables, tm):
    m = h.shape[0]
    tkc = min(TK, tm)
    n_tab = tables[0].shape[0] // tm
    tab = lambda i: (i % n_tab, 0)
    const = lambda i: (0, 0)
    return pl.pallas_call(
        _gqa_proj_body,
        grid=(m // tm,),
        in_specs=[
            pl.BlockSpec((tm, D), lambda i: (i, 0)),
            pl.BlockSpec((1, D), const),
            pl.BlockSpec((D, D), const),
            pl.BlockSpec((D, NKV), const),
            pl.BlockSpec((D, NKV), const),
            pl.BlockSpec((NKV, D), const),
            pl.BlockSpec((1, LANES), const),
            pl.BlockSpec((1, LANES), const),
            pl.BlockSpec((tm, LANES), tab),
            pl.BlockSpec((tm, LANES), tab),
            pl.BlockSpec((tm, LANES), tab),
        ],
        out_specs=[
            pl.BlockSpec((tm, D), lambda i: (i, 0)),
            pl.BlockSpec((tm, NKV), lambda i: (i, 0)),
            pl.BlockSpec((tm, NKV), lambda i: (i, 0)),
            pl.BlockSpec((tm // tkc, KV_HEADS * VROWS, tkc), lambda i: (i, 0, 0)),
        ],
        out_shape=[
            jax.ShapeDtypeStruct((m, D), BF16),
            jax.ShapeDtypeStruct((m, NKV), BF16),
            jax.ShapeDtypeStruct((m, NKV), BF16),
            jax.ShapeDtypeStruct((m // tkc, KV_HEADS * VROWS, tkc), BF16),
        ],
        compiler_params=_params(("parallel",), 48),
        name="gqa_proj",
    )(h, g, wq, wk, wv, wvt, qg, kg, *tables)


def _rope_tables():
    t = np.arange(SEQ)
    row = (t // GRID_W).astype(np.float32)
    col = (t % GRID_W).astype(np.float32)
    sec = DH // 2
    freqs = ROPE_THETA ** (-jnp.arange(0, sec, 2, dtype=F32) / sec)
    ang = jnp.stack([row[:, None] * freqs, col[:, None] * freqs], axis=1)
    cos = jnp.cos(ang)
    sin = jnp.sin(ang)
    zero = jnp.zeros_like(sin)
    cos_h = jnp.stack([cos, cos], axis=2).reshape(SEQ, DH)
    sa_h = jnp.stack([-sin, zero], axis=2).reshape(SEQ, DH)
    sb_h = jnp.stack([zero, sin], axis=2).reshape(SEQ, DH)
    reps = LANES // DH
    return tuple(jnp.tile(a, (1, reps)) for a in (cos_h, sa_h, sb_h))


def _identity_rope_tables(m):
    return (jnp.ones((m, LANES), F32), jnp.zeros((m, LANES), F32), jnp.zeros((m, LANES), F32))


def _gqa_body(q_ref, k_ref, vt_ref, vtm_ref, o_ref, qt_ref, m_ref, acc_ref, *, shifted):
    row_lo = lax.broadcasted_iota(jnp.int32, (LANES, TQ), 0) < DH

    def head_slices(hd):
        s, half = divmod(hd, 2)
        j = s // 4
        g = 2 * j + half
        return slice(j * LANES, (j + 1) * LANES), slice(g * VROWS, (g + 1) * VROWS)

    for s in range(N_SLABS):
        qt = q_ref[:, s * LANES:(s + 1) * LANES].astype(F32).T
        zero = jnp.zeros_like(qt)
        qt_ref[2 * s] = jnp.where(row_lo, qt, zero).astype(BF16)
        qt_ref[2 * s + 1] = jnp.where(row_lo, zero, qt).astype(BF16)

    def chunk(c, first, last):
        rows = TK + N_META if last else TK
        off = c * TK if isinstance(c, int) else pl.multiple_of(c * TK, TK)

        def scores(hd):
            ksl, _ = head_slices(hd)
            return _dot(k_ref[pl.ds(off, rows), ksl], qt_ref[hd])

        pending = scores(0)
        for hd in range(HEADS):
            st = pending
            if hd + 1 < HEADS:
                pending = scores(hd + 1)
            _, vsl = head_slices(hd)
            if shifted:
                m_new = jnp.max(st, axis=0, keepdims=True)
                if not first:
                    m_old = m_ref[hd]
                    m_new = jnp.maximum(m_old, m_new)
                m_ref[hd] = m_new
                st = st - m_new
            p = jnp.exp2(st).astype(BF16)
            pv = _dot(vt_ref[c, vsl, :], p[:TK])
            if last:
                pv = pv + _dot(vtm_ref[vsl, :], p[TK:])
            if first:
                acc_ref[hd] = pv
            elif shifted:
                acc_ref[hd] = jnp.exp2(m_old - m_new) * acc_ref[hd] + pv
            else:
                acc_ref[hd] += pv

    def middle(c, carry):
        chunk(c, False, False)
        return carry

    chunk(0, True, False)
    lax.fori_loop(1, NKC - 1, middle, 0)
    chunk(NKC - 1, False, True)

    for s in range(N_SLABS):
        a = acc_ref[2 * s]
        b = acc_ref[2 * s + 1]
        ot = jnp.concatenate([a[:DH] / a[DH:DH + 1], b[:DH] / b[DH:DH + 1]], axis=0)
        o_ref[:, s * LANES:(s + 1) * LANES] = ot.T.astype(BF16)


def _gqa_attn(q, k_all, vt, vt_meta, shifted):
    return pl.pallas_call(
        functools.partial(_gqa_body, shifted=shifted),
        grid=(B, SEQ // TQ),
        in_specs=[
            pl.BlockSpec((TQ, D), lambda b, i: (b * (SEQ // TQ) + i, 0)),
            pl.BlockSpec((None, SEQ + N_META, NKV), lambda b, i: (b, 0, 0)),
            pl.BlockSpec((NKC, KV_HEADS * VROWS, TK), lambda b, i: (b, 0, 0)),
            pl.BlockSpec((None, KV_HEADS * VROWS, N_META), lambda b, i: (b, 0, 0)),
        ],
        out_specs=pl.BlockSpec((TQ, D), lambda b, i: (b * (SEQ // TQ) + i, 0)),
        out_shape=jax.ShapeDtypeStruct((M_X, D), BF16),
        scratch_shapes=[
            pltpu.VMEM((HEADS, LANES, TQ), BF16),
            pltpu.VMEM((HEADS, 1, TQ), F32),
            pltpu.VMEM((HEADS, VROWS, TQ), F32),
        ],
        compiler_params=_params(("parallel", "arbitrary"), 48),
        name="gqa_attn",
    )(q, k_all, vt, vt_meta)


def _gqa_meta_body(qm_ref, k_ref, km_ref, v_ref, vm_ref, om_ref):
    lo_m = lax.broadcasted_iota(jnp.int32, (N_META, LANES), 1) < DH
    slabs_per_pair = N_SLABS // (NKV // LANES)
    for j in range(NKV // LANES):
        ksl = slice(j * LANES, (j + 1) * LANES)
        qs = jnp.concatenate(
            [_split_heads(qm_ref[:, s * LANES:(s + 1) * LANES], lo_m)
             for s in range(j * slabs_per_pair, (j + 1) * slabs_per_pair)], axis=0)
        s_m = _dot_nt(qs, km_ref[:, ksl])
        s_x = _dot_nt(qs, k_ref[:, ksl])
        m = jnp.maximum(jnp.max(s_m, axis=-1, keepdims=True), jnp.max(s_x, axis=-1, keepdims=True))
        p_m = jnp.exp2(s_m - m)
        p_x = jnp.exp2(s_x - m)
        l = jnp.sum(p_m, axis=-1, keepdims=True) + jnp.sum(p_x, axis=-1, keepdims=True)
        o = (_dot(p_m.astype(BF16), vm_ref[:, ksl]) + _dot(p_x.astype(BF16), v_ref[:, ksl])) / l
        for i in range(slabs_per_pair):
            s = j * slabs_per_pair + i
            r0 = 2 * N_META * i
            om_ref[:, s * LANES:(s + 1) * LANES] = jnp.where(
                lo_m, o[r0:r0 + N_META], o[r0 + N_META:r0 + 2 * N_META]).astype(BF16)


def _gqa_meta_attn(q_m, k_x, k_m, v_x, v_m):
    return pl.pallas_call(
        _gqa_meta_body,
        grid=(B,),
        in_specs=[
            pl.BlockSpec((N_META, D), lambda b: (b, 0)),
            pl.BlockSpec((SEQ, NKV), lambda b: (b, 0)),
            pl.BlockSpec((N_META, NKV), lambda b: (b, 0)),
            pl.BlockSpec((SEQ, NKV), lambda b: (b, 0)),
            pl.BlockSpec((N_META, NKV), lambda b: (b, 0)),
        ],
        out_specs=pl.BlockSpec((N_META, D), lambda b: (b, 0)),
        out_shape=jax.ShapeDtypeStruct((M_META, D), BF16),
        compiler_params=_params(("parallel",), 48),
        name="gqa_meta_attn",
    )(q_m, k_x, k_m, v_x, v_m)


@jax.jit
def kernel(x, meta_tokens, norm_gains, ffn_w_gate, ffn_w_up, ffn_w_down, na_w_qkv, na_w_o, na_rpb, na_meta_bias,
           gqa_w_qkv, gqa_w_o, gqa_q_gain, gqa_k_gain):
    hx = x.astype(F32).reshape(M_X, D)
    hm = jnp.broadcast_to(meta_tokens[None].astype(F32), (B, N_META, D)).reshape(M_META, D)
    streams = [(hx, TM), (hm, M_META)]
    gains = norm_gains.astype(F32).reshape(DEPTH, 6, 1, D)
    wg = ffn_w_gate.astype(BF16)
    wu = ffn_w_up.astype(BF16)
    wd = ffn_w_down.astype(BF16)
    nq = HEADS * DH

    for i in range(DEPTH):
        g = gains[i]
        j = i // 2
        streams = [(_ffn(h, g[0], g[1], wg[i, 0], wu[i, 0], wd[i, 0], tm), tm) for h, tm in streams]
        if i % 2 == 0:
            w = na_w_qkv[j].astype(BF16)
            qkv_x, qkv_m = [_na_proj(h, g[2], w, tm) for h, tm in streams]
            outs = _na_attn(qkv_x, qkv_m, _na_bias_table(na_rpb[j].astype(F32) * LOG2E),
                            _na_meta_bias_table(na_meta_bias[j].astype(F32) * LOG2E))
            w_o = na_w_o[j].astype(BF16)
        else:
            w = gqa_w_qkv[j]
            wq = w[:, :nq].reshape(D, HEADS, DH)[:, _Q_HEAD_ORDER].reshape(D, nq).astype(BF16)
            wk = w[:, nq:nq + NKV].astype(BF16)
            wv = w[:, nq + NKV:].astype(BF16)
            qg = jnp.tile(gqa_q_gain[j].astype(F32), LANES // DH).reshape(1, LANES)
            kg = jnp.tile(gqa_k_gain[j].astype(F32), LANES // DH).reshape(1, LANES)
            tables = [_rope_tables(), _identity_rope_tables(M_META)]
            (q_x, k_x, v_x, vt_x), (q_m, k_m, v_m, vt_m) = [
                _gqa_proj(h, g[2], wq, wk, wv, wv.T, qg, kg, tab, tm) for (h, tm), tab in zip(streams, tables)]
            k_all = jnp.concatenate([k_x.reshape(B, SEQ, NKV), k_m.reshape(B, N_META, NKV)], axis=1)
            vt_meta = vt_m[0].reshape(KV_HEADS * VROWS, B, N_META).transpose(1, 0, 2)
            bound = DH ** 0.5 * LOG2E * jnp.max(jnp.abs(gqa_q_gain[j])) * jnp.max(jnp.abs(gqa_k_gain[j]))
            o_x = lax.cond(bound <= MAX_DIRECT_EXPONENT,
                           functools.partial(_gqa_attn, shifted=False),
                           functools.partial(_gqa_attn, shifted=True),
                           q_x, k_all, vt_x, vt_meta)
            outs = (o_x, _gqa_meta_attn(q_m, k_x, k_m, v_x, v_m))
            w_o = gqa_w_o[j].reshape(HEADS, DH, D)[_Q_HEAD_ORDER].reshape(nq, D).astype(BF16)
        streams = [(_oproj(o, w_o, h, g[3], tm), tm) for o, (h, tm) in zip(outs, streams)]
        if i == DEPTH - 1:
            streams = streams[:1]
        streams = [(_ffn(h, g[4], g[5], wg[i, 1], wu[i, 1], wd[i, 1], tm), tm) for h, tm in streams]
    return streams[0][0].reshape(B, SEQ, D)
```

```python
import functools
import math

import numpy as np
import jax
import jax.numpy as jnp
from jax import lax
from jax.experimental import pallas as pl
from jax.experimental.pallas import tpu as pltpu

D = 1024
B = 16
SEQ = 4096
DEPTH = 2
GRID_W = 64
ROWS = SEQ // GRID_W
N_META = 16
HEADS = 16
DH = 64
WIN_R = 8
WIN_C = 16
KV_HEADS = 4
NKV = KV_HEADS * DH
D_FF = 2816
EPS = 1e-6
ROPE_THETA = 10000.0

LANES = 128
N_SLABS = D // LANES
M_X = B * SEQ
M_META = B * N_META
TM = 1024
FFN_SUB = 2
OPROJ_SUB = 4
TF = 256
NF = D_FF // TF
BAND = WIN_R * GRID_W
TQ = 512
TK = 512
NKC = SEQ // TK
NEG = -1e30
MAX_DIRECT_EXPONENT = 40.0
LOG2E = math.log2(math.e)

F32 = jnp.float32
BF16 = jnp.bfloat16


def _dot(a, b):
    return jnp.dot(a, b, preferred_element_type=F32)


def _dot_nt(a, b):
    return lax.dot_general(a, b, (((1,), (1,)), ((), ())), preferred_element_type=F32)


def _params(sem, vmem_mib):
    return pltpu.CompilerParams(dimension_semantics=sem, vmem_limit_bytes=vmem_mib * 2 ** 20)


def _rms(x, gain):
    ms = jnp.mean(x * x, axis=-1, keepdims=True)
    return x * lax.rsqrt(ms + EPS) * gain


def _split_heads(q128, lo):
    zero = jnp.zeros_like(q128)
    return jnp.concatenate([jnp.where(lo, q128, zero), jnp.where(lo, zero, q128)], axis=0)


def _ffn_body(h_ref, gpre_ref, gpost_ref, wg_ref, wu_ref, wd_ref, o_ref, xn_ref, acc_ref):
    tm = h_ref.shape[0]
    n_sub = FFN_SUB if tm % (FFN_SUB * LANES) == 0 else 1
    sub = tm // n_sub

    def rows(r):
        return slice(r * sub, (r + 1) * sub)

    def prologue(r):
        xn_ref[rows(r)] = _rms(h_ref[rows(r)], gpre_ref[...]).astype(BF16)

    def gate_up(r, f):
        xn = xn_ref[rows(r)]
        cols = slice(f * TF, (f + 1) * TF)
        return _dot(xn, wg_ref[:, cols]), _dot(xn, wu_ref[:, cols])

    def epilogue(r):
        o_ref[rows(r)] = h_ref[rows(r)] + 0.5 * _rms(acc_ref[rows(r)], gpost_ref[...])

    units = [(r, f) for r in range(n_sub) for f in range(NF)]
    prologue(0)
    pending = gate_up(*units[0])
    for idx, (r, f) in enumerate(units):
        g, u = pending
        if f == 0 and r + 1 < n_sub:
            prologue(r + 1)
        if idx + 1 < len(units):
            pending = gate_up(*units[idx + 1])
        a = (g * jax.nn.sigmoid(g) * u).astype(BF16)
        part = _dot(a, wd_ref[f * TF:(f + 1) * TF, :])
        if f == 0:
            acc_ref[rows(r)] = part
        else:
            acc_ref[rows(r)] += part
        if f == NF - 1:
            epilogue(r)


def _ffn(h, gpre, gpost, wg, wu, wd, tm):
    const = lambda i: (0, 0)
    resident = pl.Buffered(1)
    return pl.pallas_call(
        _ffn_body,
        grid=(h.shape[0] // tm,),
        in_specs=[
            pl.BlockSpec((tm, D), lambda i: (i, 0)),
            pl.BlockSpec((1, D), const),
            pl.BlockSpec((1, D), const),
            pl.BlockSpec((D, D_FF), const, pipeline_mode=resident),
            pl.BlockSpec((D, D_FF), const, pipeline_mode=resident),
            pl.BlockSpec((D_FF, D), const, pipeline_mode=resident),
        ],
        out_specs=pl.BlockSpec((tm, D), lambda i: (i, 0)),
        out_shape=jax.ShapeDtypeStruct(h.shape, F32),
        scratch_shapes=[pltpu.VMEM((tm, D), BF16), pltpu.VMEM((tm, D), F32)],
        compiler_params=_params(("parallel",), 56),
        name="ffn",
    )(h, gpre, gpost, wg, wu, wd)


def _oproj_body(o_ref, w_ref, h_ref, g_ref, out_ref):
    tm = o_ref.shape[0]
    n_sub = OPROJ_SUB if tm % (OPROJ_SUB * LANES) == 0 else 1
    sub = tm // n_sub

    def project(r):
        return _dot(o_ref[r * sub:(r + 1) * sub], w_ref[...])

    pending = project(0)
    for r in range(n_sub):
        m = pending
        if r + 1 < n_sub:
            pending = project(r + 1)
        rows = slice(r * sub, (r + 1) * sub)
        out_ref[rows] = h_ref[rows] + _rms(m, g_ref[...])


def _oproj(o, w, h, g, tm):
    return pl.pallas_call(
        _oproj_body,
        grid=(h.shape[0] // tm,),
        in_specs=[
            pl.BlockSpec((tm, D), lambda i: (i, 0)),
            pl.BlockSpec((D, D), lambda i: (0, 0)),
            pl.BlockSpec((tm, D), lambda i: (i, 0)),
            pl.BlockSpec((1, D), lambda i: (0, 0)),
        ],
        out_specs=pl.BlockSpec((tm, D), lambda i: (i, 0)),
        out_shape=jax.ShapeDtypeStruct(h.shape, F32),
        compiler_params=_params(("parallel",), 48),
        name="oproj",
    )(o, w, h, g)


def _na_proj_body(h_ref, g_ref, w_ref, o_ref):
    xn = _rms(h_ref[...], g_ref[...]).astype(BF16)
    scales = (DH ** -0.5 * LOG2E, 1.0, 1.0)
    for n, scale in enumerate(scales):
        cols = slice(n * D, (n + 1) * D)
        o_ref[:, cols] = (_dot(xn, w_ref[:, cols]) * scale).astype(BF16)


def _na_proj(h, g, w, tm):
    return pl.pallas_call(
        _na_proj_body,
        grid=(h.shape[0] // tm,),
        in_specs=[
            pl.BlockSpec((tm, D), lambda i: (i, 0)),
            pl.BlockSpec((1, D), lambda i: (0, 0)),
            pl.BlockSpec((D, 3 * D), lambda i: (0, 0), pipeline_mode=pl.Buffered(1)),
        ],
        out_specs=pl.BlockSpec((tm, 3 * D), lambda i: (i, 0)),
        out_shape=jax.ShapeDtypeStruct((h.shape[0], 3 * D), BF16),
        compiler_params=_params(("parallel",), 48),
        name="na_proj",
    )(h, g, w)


NA_RB = 4


def _na_body(q_ref, k_ref, v_ref, qm_ref, km_ref, vm_ref, *refs):
    bias_refs = refs[:NA_RB]
    mb_ref, o_ref, om_ref = refs[NA_RB:]
    rb = pl.program_id(1)
    lo = lax.broadcasted_iota(jnp.int32, (GRID_W, LANES), 1) < DH
    ones = jnp.ones((BAND, LANES), BF16)
    ones_m = jnp.ones((N_META, LANES), BF16)

    def band_start(rr):
        rs = jnp.clip(rb * NA_RB + rr - WIN_R // 2, 0, ROWS - WIN_R)
        return pl.multiple_of(rs * GRID_W, GRID_W)

    def scores(t):
        rr, s = divmod(t, N_SLABS)
        sl = slice(s * LANES, (s + 1) * LANES)
        qs = _split_heads(q_ref[rr * GRID_W:(rr + 1) * GRID_W, sl], lo)
        sc = _dot_nt(qs, k_ref[pl.ds(band_start(rr), BAND), sl]) \
            + bias_refs[rr][2 * s:2 * s + 2].reshape(2 * GRID_W, BAND)
        sm = _dot_nt(qs, km_ref[:, sl]) + mb_ref[s]
        return sm, sc

    def finish(t, sm, sc):
        rr, s = divmod(t, N_SLABS)
        sl = slice(s * LANES, (s + 1) * LANES)
        m = jnp.maximum(jnp.max(sm, axis=-1, keepdims=True), jnp.max(sc, axis=-1, keepdims=True))
        vb = jnp.concatenate([v_ref[pl.ds(band_start(rr), BAND), sl], ones], axis=1)
        vm = jnp.concatenate([vm_ref[:, sl], ones_m], axis=1)
        ol = _dot(jnp.exp2(sc - m).astype(BF16), vb) + _dot(jnp.exp2(sm - m).astype(BF16), vm)
        o = ol[:, :LANES] / ol[:, LANES:]
        o_ref[rr * GRID_W:(rr + 1) * GRID_W, sl] = jnp.where(lo, o[:GRID_W], o[GRID_W:]).astype(BF16)

    n = NA_RB * N_SLABS
    queue = [scores(0), scores(1)]
    for t in range(n):
        sm, sc = queue.pop(0)
        if t + 2 < n:
            queue.append(scores(t + 2))
        finish(t, sm, sc)

    @pl.when(rb == 0)
    def _():
        lo_m = lax.broadcasted_iota(jnp.int32, (N_META, LANES), 1) < DH
        for s in range(N_SLABS):
            sl = slice(s * LANES, (s + 1) * LANES)
            qs = _split_heads(qm_ref[:, sl], lo_m)
            st = _dot_nt(qs, km_ref[:, sl])
            p = jnp.exp2(st - jnp.max(st, axis=-1, keepdims=True))
            o = _dot(p.astype(BF16), vm_ref[:, sl]) / jnp.sum(p, axis=-1, keepdims=True)
            om_ref[:, sl] = jnp.where(lo_m, o[:N_META], o[N_META:]).astype(BF16)


def _na_bias_row(r):
    rs = jnp.clip(r - WIN_R // 2, 0, ROWS - WIN_R)
    return rs - r + (WIN_R - 1)


def _na_attn(qkv_x, qkv_m, bias, mbias):
    nrb = ROWS // NA_RB
    once = pl.Buffered(1)
    bias_specs = [
        pl.BlockSpec((HEADS, None, GRID_W, BAND), lambda b, r, rr=rr: (0, _na_bias_row(NA_RB * r + rr), 0, 0))
        for rr in range(NA_RB)]
    return pl.pallas_call(
        _na_body,
        grid=(B, nrb),
        in_specs=[
            pl.BlockSpec((NA_RB * GRID_W, D), lambda b, r: (b * nrb + r, 0)),
            pl.BlockSpec((SEQ, D), lambda b, r: (b, 1), pipeline_mode=once),
            pl.BlockSpec((SEQ, D), lambda b, r: (b, 2), pipeline_mode=once),
            pl.BlockSpec((N_META, D), lambda b, r: (b, 0)),
            pl.BlockSpec((N_META, D), lambda b, r: (b, 1)),
            pl.BlockSpec((N_META, D), lambda b, r: (b, 2)),
            *bias_specs,
            pl.BlockSpec((N_SLABS, 2 * GRID_W, N_META), lambda b, r: (0, 0, 0)),
        ],
        out_specs=[
            pl.BlockSpec((NA_RB * GRID_W, D), lambda b, r: (b * nrb + r, 0)),
            pl.BlockSpec((N_META, D), lambda b, r: (b, 0)),
        ],
        out_shape=[
            jax.ShapeDtypeStruct((M_X, D), BF16),
            jax.ShapeDtypeStruct((M_META, D), BF16),
        ],
        compiler_params=_params(("parallel", "arbitrary"), 56),
        name="na_attn",
    )(qkv_x, qkv_x, qkv_x, qkv_m, qkv_m, qkv_m, *([bias] * NA_RB), mbias)


def _na_bias_table(rpb):
    cols = np.arange(GRID_W)
    col_start = np.clip(cols - WIN_C // 2, 0, GRID_W - WIN_C)
    kc = np.arange(GRID_W)
    valid = (kc[None, :] >= col_start[:, None]) & (kc[None, :] < col_start[:, None] + WIN_C)
    col_rel = kc[None, :] - cols[:, None] + (WIN_C - 1)
    onehot = np.zeros((GRID_W, GRID_W, 2 * WIN_C - 1), np.float32)
    ci, ki = np.nonzero(valid)
    onehot[ci, ki, col_rel[ci, ki]] = 1.0
    base = jnp.einsum("hrx,ckx->hrck", rpb.astype(F32), onehot, precision=lax.Precision.HIGHEST)
    base = jnp.where(valid[None, None], base, NEG)
    t = jnp.stack([base[:, d:d + WIN_R] for d in range(WIN_R)], axis=1)
    return jnp.transpose(t, (0, 1, 3, 2, 4)).reshape(HEADS, WIN_R, GRID_W, BAND)


def _na_meta_bias_table(meta_bias):
    mb = meta_bias.astype(F32).reshape(N_SLABS, 2, 1, N_META)
    return jnp.broadcast_to(mb, (N_SLABS, 2, GRID_W, N_META)).reshape(N_SLABS, 2 * GRID_W, N_META)


_Q_HEAD_ORDER = np.array([8 * j + 4 * hi + i for j in range(2) for i in range(4) for hi in range(2)])


def _head_norm_rope(x, gain, cos, sin_a, sin_b, lo):
    sq = x * x
    s_lo = jnp.sum(jnp.where(lo, sq, 0.0), axis=-1, keepdims=True)
    s_hi = jnp.sum(jnp.where(lo, 0.0, sq), axis=-1, keepdims=True)
    inv = lax.rsqrt(jnp.where(lo, s_lo, s_hi) * (1.0 / DH) + EPS)
    y = x * inv * gain
    quarter = DH // 4
    return y * cos + pltpu.roll(y, LANES - quarter, 1) * sin_a + pltpu.roll(y, quarter, 1) * sin_b


def _gqa_proj_body(h_ref, g_ref, wq_ref, wk_ref, wv_ref, wvt_ref, qg_ref, kg_ref, cos_ref, sa_ref, sb_ref,
                   q_ref, k_ref, v_ref, vt_ref):
    tm = h_ref.shape[0]
    tkc = vt_ref.shape[2]
    xn = _rms(h_ref[...], g_ref[...]).astype(BF16)
    qf = _dot(xn, wq_ref[...])
    kf = _dot(xn, wk_ref[...])
    v_ref[...] = _dot(xn, wv_ref[...]).astype(BF16)
    vt = _dot_nt(wvt_ref[...], xn).astype(BF16)
    for c in range(tm // tkc):
        vt_ref[c] = vt[:, c * tkc:(c + 1) * tkc]
    lo = lax.broadcasted_iota(jnp.int32, (tm, LANES), 1) < DH
    cos, sa, sb = cos_ref[...], sa_ref[...], sb_ref[...]
    for s in range(N_SLABS):
        sl = slice(s * LANES, (s + 1) * LANES)
        q = _head_norm_rope(qf[:, sl], qg_ref[...], cos, sa, sb, lo)
        q_ref[:, sl] = (q * (DH ** -0.5 * LOG2E)).astype(BF16)
    for s in range(NKV // LANES):
        sl = slice(s * LANES, (s + 1) * LANES)
        k_ref[:, sl] = _head_norm_rope(kf[:, sl], kg_ref[...], cos, sa, sb, lo).astype(BF16)


def _gqa_proj(h, g, wq, wk, wv, wvt, qg, kg, tables, tm):
    m = h.shape[0]
    tkc = min(TK, tm)
    n_tab = tables[0].shape[0] // tm
    tab = lambda i: (i % n_tab, 0)
    const = lambda i: (0, 0)
    return pl.pallas_call(
        _gqa_proj_body,
        grid=(m // tm,),
        in_specs=[
            pl.BlockSpec((tm, D), lambda i: (i, 0)),
            pl.BlockSpec((1, D), const),
            pl.BlockSpec((D, D), const),
            pl.BlockSpec((D, NKV), const),
            pl.BlockSpec((D, NKV), const),
            pl.BlockSpec((NKV, D), const),
            pl.BlockSpec((1, LANES), const),
            pl.BlockSpec((1, LANES), const),
            pl.BlockSpec((tm, LANES), tab),
            pl.BlockSpec((tm, LANES), tab),
            pl.BlockSpec((tm, LANES), tab),
        ],
        out_specs=[
            pl.BlockSpec((tm, D), lambda i: (i, 0)),
            pl.BlockSpec((tm, NKV), lambda i: (i, 0)),
            pl.BlockSpec((tm, NKV), lambda i: (i, 0)),
            pl.BlockSpec((tm // tkc, NKV, tkc), lambda i: (i, 0, 0)),
        ],
        out_shape=[
            jax.ShapeDtypeStruct((m, D), BF16),
            jax.ShapeDtypeStruct((m, NKV), BF16),
            jax.ShapeDtypeStruct((m, NKV), BF16),
            jax.ShapeDtypeStruct((m // tkc, NKV, tkc), BF16),
        ],
        compiler_params=_params(("parallel",), 48),
        name="gqa_proj",
    )(h, g, wq, wk, wv, wvt, qg, kg, *tables)


def _rope_tables():
    t = np.arange(SEQ)
    row = (t // GRID_W).astype(np.float32)
    col = (t % GRID_W).astype(np.float32)
    sec = DH // 2
    freqs = ROPE_THETA ** (-jnp.arange(0, sec, 2, dtype=F32) / sec)
    ang = jnp.stack([row[:, None] * freqs, col[:, None] * freqs], axis=1)
    cos = jnp.cos(ang)
    sin = jnp.sin(ang)
    zero = jnp.zeros_like(sin)
    cos_h = jnp.stack([cos, cos], axis=2).reshape(SEQ, DH)
    sa_h = jnp.stack([-sin, zero], axis=2).reshape(SEQ, DH)
    sb_h = jnp.stack([zero, sin], axis=2).reshape(SEQ, DH)
    reps = LANES // DH
    return tuple(jnp.tile(a, (1, reps)) for a in (cos_h, sa_h, sb_h))


def _identity_rope_tables(m):
    return (jnp.ones((m, LANES), F32), jnp.zeros((m, LANES), F32), jnp.zeros((m, LANES), F32))


def _gqa_body(q_ref, k_ref, vt_ref, vtm_ref, o_ref, qt_ref, m_ref, acc_ref, l_ref, *, shifted):
    row_lo = lax.broadcasted_iota(jnp.int32, (LANES, TQ), 0) < DH

    def head_slices(hd):
        s, half = divmod(hd, 2)
        j = s // 4
        g = 2 * j + half
        return slice(j * LANES, (j + 1) * LANES), slice(g * DH, (g + 1) * DH)

    for s in range(N_SLABS):
        qt = q_ref[:, s * LANES:(s + 1) * LANES].astype(F32).T
        zero = jnp.zeros_like(qt)
        qt_ref[2 * s] = jnp.where(row_lo, qt, zero).astype(BF16)
        qt_ref[2 * s + 1] = jnp.where(row_lo, zero, qt).astype(BF16)

    def chunk(c, first, last):
        rows = TK + N_META if last else TK
        off = c * TK if isinstance(c, int) else pl.multiple_of(c * TK, TK)

        def scores(hd):
            ksl, _ = head_slices(hd)
            return _dot(k_ref[pl.ds(off, rows), ksl], qt_ref[hd])

        queue = [scores(0), scores(1)]
        for hd in range(HEADS):
            st = queue.pop(0)
            if hd + 2 < HEADS:
                queue.append(scores(hd + 2))
            _, vsl = head_slices(hd)
            if shifted:
                m_new = jnp.max(st, axis=0, keepdims=True)
                if not first:
                    m_old = m_ref[hd]
                    m_new = jnp.maximum(m_old, m_new)
                m_ref[hd] = m_new
                st = st - m_new
            pf = jnp.exp2(st)
            lsum = jnp.sum(pf, axis=0, keepdims=True)
            p = pf.astype(BF16)
            pv = _dot(vt_ref[c, vsl, :], p[:TK])
            if last:
                pv = pv + _dot(vtm_ref[vsl, :], p[TK:])
            if first:
                acc_ref[hd] = pv
                l_ref[hd] = lsum
            elif shifted:
                alpha = jnp.exp2(m_old - m_new)
                acc_ref[hd] = alpha * acc_ref[hd] + pv
                l_ref[hd] = alpha * l_ref[hd] + lsum
            else:
                acc_ref[hd] += pv
                l_ref[hd] += lsum

    def middle(c, carry):
        chunk(c, False, False)
        return carry

    chunk(0, True, False)
    lax.fori_loop(1, NKC - 1, middle, 0)
    chunk(NKC - 1, False, True)

    for s in range(N_SLABS):
        ot = jnp.concatenate([acc_ref[2 * s] / l_ref[2 * s], acc_ref[2 * s + 1] / l_ref[2 * s + 1]], axis=0)
        o_ref[:, s * LANES:(s + 1) * LANES] = ot.T.astype(BF16)


def _gqa_attn(q, k_all, vt, vt_meta, shifted):
    return pl.pallas_call(
        functools.partial(_gqa_body, shifted=shifted),
        grid=(B, SEQ // TQ),
        in_specs=[
            pl.BlockSpec((TQ, D), lambda b, i: (b * (SEQ // TQ) + i, 0)),
            pl.BlockSpec((None, SEQ + N_META, NKV), lambda b, i: (b, 0, 0)),
            pl.BlockSpec((NKC, NKV, TK), lambda b, i: (b, 0, 0)),
            pl.BlockSpec((None, NKV, N_META), lambda b, i: (b, 0, 0)),
        ],
        out_specs=pl.BlockSpec((TQ, D), lambda b, i: (b * (SEQ // TQ) + i, 0)),
        out_shape=jax.ShapeDtypeStruct((M_X, D), BF16),
        scratch_shapes=[
            pltpu.VMEM((HEADS, LANES, TQ), BF16),
            pltpu.VMEM((HEADS, 1, TQ), F32),
            pltpu.VMEM((HEADS, DH, TQ), F32),
            pltpu.VMEM((HEADS, 1, TQ), F32),
        ],
        compiler_params=_params(("parallel", "arbitrary"), 48),
        name="gqa_attn",
    )(q, k_all, vt, vt_meta)


def _gqa_meta_body(qm_ref, k_ref, km_ref, v_ref, vm_ref, om_ref):
    lo_m = lax.broadcasted_iota(jnp.int32, (N_META, LANES), 1) < DH
    slabs_per_pair = N_SLABS // (NKV // LANES)
    for j in range(NKV // LANES):
        ksl = slice(j * LANES, (j + 1) * LANES)
        qs = jnp.concatenate(
            [_split_heads(qm_ref[:, s * LANES:(s + 1) * LANES], lo_m)
             for s in range(j * slabs_per_pair, (j + 1) * slabs_per_pair)], axis=0)
        s_m = _dot_nt(qs, km_ref[:, ksl])
        s_x = _dot_nt(qs, k_ref[:, ksl])
        m = jnp.maximum(jnp.max(s_m, axis=-1, keepdims=True), jnp.max(s_x, axis=-1, keepdims=True))
        p_m = jnp.exp2(s_m - m)
        p_x = jnp.exp2(s_x - m)
        l = jnp.sum(p_m, axis=-1, keepdims=True) + jnp.sum(p_x, axis=-1, keepdims=True)
        o = (_dot(p_m.astype(BF16), vm_ref[:, ksl]) + _dot(p_x.astype(BF16), v_ref[:, ksl])) / l
        for i in range(slabs_per_pair):
            s = j * slabs_per_pair + i
            r0 = 2 * N_META * i
            om_ref[:, s * LANES:(s + 1) * LANES] = jnp.where(
                lo_m, o[r0:r0 + N_META], o[r0 + N_META:r0 + 2 * N_META]).astype(BF16)


def _gqa_meta_attn(q_m, k_x, k_m, v_x, v_m):
    return pl.pallas_call(
        _gqa_meta_body,
        grid=(B,),
        in_specs=[
            pl.BlockSpec((N_META, D), lambda b: (b, 0)),
            pl.BlockSpec((SEQ, NKV), lambda b: (b, 0)),
            pl.BlockSpec((N_META, NKV), lambda b: (b, 0)),
            pl.BlockSpec((SEQ, NKV), lambda b: (b, 0)),
            pl.BlockSpec((N_META, NKV), lambda b: (b, 0)),
        ],
        out_specs=pl.BlockSpec((N_META, D), lambda b: (b, 0)),
        out_shape=jax.ShapeDtypeStruct((M_META, D), BF16),
        compiler_params=_params(("parallel",), 48),
        name="gqa_meta_attn",
    )(q_m, k_x, k_m, v_x, v_m)


@jax.jit
def kernel(x, meta_tokens, norm_gains, ffn_w_gate, ffn_w_up, ffn_w_down, na_w_qkv, na_w_o, na_rpb, na_meta_bias,
           gqa_w_qkv, gqa_w_o, gqa_q_gain, gqa_k_gain):
    hx = x.astype(F32).reshape(M_X, D)
    hm = jnp.broadcast_to(meta_tokens[None].astype(F32), (B, N_META, D)).reshape(M_META, D)
    streams = [(hx, TM), (hm, M_META)]
    gains = norm_gains.astype(F32).reshape(DEPTH, 6, 1, D)
    wg = ffn_w_gate.astype(BF16)
    wu = ffn_w_up.astype(BF16)
    wd = ffn_w_down.astype(BF16)
    nq = HEADS * DH

    for i in range(DEPTH):
        g = gains[i]
        j = i // 2
        streams = [(_ffn(h, g[0], g[1], wg[i, 0], wu[i, 0], wd[i, 0], tm), tm) for h, tm in streams]
        if i % 2 == 0:
            w = na_w_qkv[j].astype(BF16)
            qkv_x, qkv_m = [_na_proj(h, g[2], w, tm) for h, tm in streams]
            outs = _na_attn(qkv_x, qkv_m, _na_bias_table(na_rpb[j].astype(F32) * LOG2E),
                            _na_meta_bias_table(na_meta_bias[j].astype(F32) * LOG2E))
            w_o = na_w_o[j].astype(BF16)
        else:
            w = gqa_w_qkv[j]
            wq = w[:, :nq].reshape(D, HEADS, DH)[:, _Q_HEAD_ORDER].reshape(D, nq).astype(BF16)
            wk = w[:, nq:nq + NKV].astype(BF16)
            wv = w[:, nq + NKV:].astype(BF16)
            qg = jnp.tile(gqa_q_gain[j].astype(F32), LANES // DH).reshape(1, LANES)
            kg = jnp.tile(gqa_k_gain[j].astype(F32), LANES // DH).reshape(1, LANES)
            tables = [_rope_tables(), _identity_rope_tables(M_META)]
            (q_x, k_x, v_x, vt_x), (q_m, k_m, v_m, vt_m) = [
                _gqa_proj(h, g[2], wq, wk, wv, wv.T, qg, kg, tab, tm) for (h, tm), tab in zip(streams, tables)]
            k_all = jnp.concatenate([k_x.reshape(B, SEQ, NKV), k_m.reshape(B, N_META, NKV)], axis=1)
            vt_meta = vt_m[0].reshape(NKV, B, N_META).transpose(1, 0, 2)
            bound = DH ** 0.5 * LOG2E * jnp.max(jnp.abs(gqa_q_gain[j])) * jnp.max(jnp.abs(gqa_k_gain[j]))
            o_x = lax.cond(bound <= MAX_DIRECT_EXPONENT,
                           functools.partial(_gqa_attn, shifted=False),
                           functools.partial(_gqa_attn, shifted=True),
                           q_x, k_all, vt_x, vt_meta)
            outs = (o_x, _gqa_meta_attn(q_m, k_x, k_m, v_x, v_m))
            w_o = gqa_w_o[j].reshape(HEADS, DH, D)[_Q_HEAD_ORDER].reshape(nq, D).astype(BF16)
        streams = [(_oproj(o, w_o, h, g[3], tm), tm) for o, (h, tm) in zip(outs, streams)]
        if i == DEPTH - 1:
            streams = streams[:1]
        streams = [(_ffn(h, g[4], g[5], wg[i, 1], wu[i, 1], wd[i, 1], tm), tm) for h, tm in streams]
    return streams[0][0].reshape(B, SEQ, D)
```

```python
import functools
import math

import numpy as np
import jax
import jax.numpy as jnp
from jax import lax
from jax.experimental import pallas as pl
from jax.experimental.pallas import tpu as pltpu

D = 1024
B = 16
SEQ = 4096
DEPTH = 2
GRID_W = 64
ROWS = SEQ // GRID_W
N_META = 16
HEADS = 16
DH = 64
WIN_R = 8
WIN_C = 16
KV_HEADS = 4
NKV = KV_HEADS * DH
D_FF = 2816
EPS = 1e-6
ROPE_THETA = 10000.0

LANES = 128
N_SLABS = D // LANES
M_X = B * SEQ
M_META = B * N_META
TM = 1024
FFN_SUB = 2
OPROJ_SUB = 4
TF = 256
NF = D_FF // TF
BAND = WIN_R * GRID_W
TQ = 512
TK = 512
NKC = SEQ // TK
NEG = -1e30
MAX_DIRECT_EXPONENT = 40.0
LOG2E = math.log2(math.e)

F32 = jnp.float32
BF16 = jnp.bfloat16


def _dot(a, b):
    return jnp.dot(a, b, preferred_element_type=F32)


def _dot_nt(a, b):
    return lax.dot_general(a, b, (((1,), (1,)), ((), ())), preferred_element_type=F32)


def _params(sem, vmem_mib):
    return pltpu.CompilerParams(dimension_semantics=sem, vmem_limit_bytes=vmem_mib * 2 ** 20)


def _rms(x, gain):
    ms = jnp.mean(x * x, axis=-1, keepdims=True)
    return x * lax.rsqrt(ms + EPS) * gain


def _split_heads(q128, lo):
    zero = jnp.zeros_like(q128)
    return jnp.concatenate([jnp.where(lo, q128, zero), jnp.where(lo, zero, q128)], axis=0)


def _ffn_body(h_ref, gpre_ref, gpost_ref, wg_ref, wu_ref, wd_ref, o_ref, xn_ref, acc_ref):
    tm = h_ref.shape[0]
    n_sub = FFN_SUB if tm % (FFN_SUB * LANES) == 0 else 1
    sub = tm // n_sub

    def rows(r):
        return slice(r * sub, (r + 1) * sub)

    def prologue(r):
        xn_ref[rows(r)] = _rms(h_ref[rows(r)], gpre_ref[...]).astype(BF16)

    def gate_up(r, f):
        xn = xn_ref[rows(r)]
        cols = slice(f * TF, (f + 1) * TF)
        return _dot(xn, wg_ref[:, cols]), _dot(xn, wu_ref[:, cols])

    def epilogue(r):
        o_ref[rows(r)] = h_ref[rows(r)] + 0.5 * _rms(acc_ref[rows(r)], gpost_ref[...])

    units = [(r, f) for r in range(n_sub) for f in range(NF)]
    prologue(0)
    pending = gate_up(*units[0])
    for idx, (r, f) in enumerate(units):
        g, u = pending
        if f == 0 and r + 1 < n_sub:
            prologue(r + 1)
        if idx + 1 < len(units):
            pending = gate_up(*units[idx + 1])
        a = (g * jax.nn.sigmoid(g) * u).astype(BF16)
        part = _dot(a, wd_ref[f * TF:(f + 1) * TF, :])
        if f == 0:
            acc_ref[rows(r)] = part
        else:
            acc_ref[rows(r)] += part
        if f == NF - 1:
            epilogue(r)


def _ffn(h, gpre, gpost, wg, wu, wd, tm):
    const = lambda i: (0, 0)
    resident = pl.Buffered(1)
    return pl.pallas_call(
        _ffn_body,
        grid=(h.shape[0] // tm,),
        in_specs=[
            pl.BlockSpec((tm, D), lambda i: (i, 0)),
            pl.BlockSpec((1, D), const),
            pl.BlockSpec((1, D), const),
            pl.BlockSpec((D, D_FF), const, pipeline_mode=resident),
            pl.BlockSpec((D, D_FF), const, pipeline_mode=resident),
            pl.BlockSpec((D_FF, D), const, pipeline_mode=resident),
        ],
        out_specs=pl.BlockSpec((tm, D), lambda i: (i, 0)),
        out_shape=jax.ShapeDtypeStruct(h.shape, F32),
        scratch_shapes=[pltpu.VMEM((tm, D), BF16), pltpu.VMEM((tm, D), F32)],
        compiler_params=_params(("parallel",), 56),
        name="ffn",
    )(h, gpre, gpost, wg, wu, wd)


def _oproj_body(o_ref, w_ref, h_ref, g_ref, out_ref):
    tm = o_ref.shape[0]
    n_sub = OPROJ_SUB if tm % (OPROJ_SUB * LANES) == 0 else 1
    sub = tm // n_sub

    def project(r):
        return _dot(o_ref[r * sub:(r + 1) * sub], w_ref[...])

    pending = project(0)
    for r in range(n_sub):
        m = pending
        if r + 1 < n_sub:
            pending = project(r + 1)
        rows = slice(r * sub, (r + 1) * sub)
        out_ref[rows] = h_ref[rows] + _rms(m, g_ref[...])


def _oproj(o, w, h, g, tm):
    return pl.pallas_call(
        _oproj_body,
        grid=(h.shape[0] // tm,),
        in_specs=[
            pl.BlockSpec((tm, D), lambda i: (i, 0)),
            pl.BlockSpec((D, D), lambda i: (0, 0)),
            pl.BlockSpec((tm, D), lambda i: (i, 0)),
            pl.BlockSpec((1, D), lambda i: (0, 0)),
        ],
        out_specs=pl.BlockSpec((tm, D), lambda i: (i, 0)),
        out_shape=jax.ShapeDtypeStruct(h.shape, F32),
        compiler_params=_params(("parallel",), 48),
        name="oproj",
    )(o, w, h, g)


def _na_proj_body(h_ref, g_ref, w_ref, o_ref):
    xn = _rms(h_ref[...], g_ref[...]).astype(BF16)
    scales = (DH ** -0.5 * LOG2E, 1.0, 1.0)
    for n, scale in enumerate(scales):
        cols = slice(n * D, (n + 1) * D)
        o_ref[:, cols] = (_dot(xn, w_ref[:, cols]) * scale).astype(BF16)


def _na_proj(h, g, w, tm):
    return pl.pallas_call(
        _na_proj_body,
        grid=(h.shape[0] // tm,),
        in_specs=[
            pl.BlockSpec((tm, D), lambda i: (i, 0)),
            pl.BlockSpec((1, D), lambda i: (0, 0)),
            pl.BlockSpec((D, 3 * D), lambda i: (0, 0), pipeline_mode=pl.Buffered(1)),
        ],
        out_specs=pl.BlockSpec((tm, 3 * D), lambda i: (i, 0)),
        out_shape=jax.ShapeDtypeStruct((h.shape[0], 3 * D), BF16),
        compiler_params=_params(("parallel",), 48),
        name="na_proj",
    )(h, g, w)


NA_RB = 4
NA_AHEAD = 3


def _na_body(q_ref, k_ref, v_ref, qm_ref, km_ref, vm_ref, *refs):
    bias_refs = refs[:NA_RB]
    mb_ref, o_ref, om_ref = refs[NA_RB:]
    rb = pl.program_id(1)
    lo = lax.broadcasted_iota(jnp.int32, (GRID_W, LANES), 1) < DH
    ones = jnp.ones((BAND, LANES), BF16)
    ones_m = jnp.ones((N_META, LANES), BF16)

    def band_start(rr):
        rs = jnp.clip(rb * NA_RB + rr - WIN_R // 2, 0, ROWS - WIN_R)
        return pl.multiple_of(rs * GRID_W, GRID_W)

    def scores(t):
        rr, s = divmod(t, N_SLABS)
        sl = slice(s * LANES, (s + 1) * LANES)
        qs = _split_heads(q_ref[rr * GRID_W:(rr + 1) * GRID_W, sl], lo)
        sc = _dot_nt(qs, k_ref[pl.ds(band_start(rr), BAND), sl]) \
            + bias_refs[rr][2 * s:2 * s + 2].reshape(2 * GRID_W, BAND)
        sm = _dot_nt(qs, km_ref[:, sl]) + mb_ref[s]
        return sm, sc

    def finish(t, sm, sc):
        rr, s = divmod(t, N_SLABS)
        sl = slice(s * LANES, (s + 1) * LANES)
        m = jnp.maximum(jnp.max(sm, axis=-1, keepdims=True), jnp.max(sc, axis=-1, keepdims=True))
        vb = jnp.concatenate([v_ref[pl.ds(band_start(rr), BAND), sl], ones], axis=1)
        vm = jnp.concatenate([vm_ref[:, sl], ones_m], axis=1)
        ol = _dot(jnp.exp2(sc - m).astype(BF16), vb) + _dot(jnp.exp2(sm - m).astype(BF16), vm)
        o = ol[:, :LANES] / ol[:, LANES:]
        o_ref[rr * GRID_W:(rr + 1) * GRID_W, sl] = jnp.where(lo, o[:GRID_W], o[GRID_W:]).astype(BF16)

    n = NA_RB * N_SLABS
    queue = [scores(t) for t in range(NA_AHEAD)]
    for t in range(n):
        sm, sc = queue.pop(0)
        if t + NA_AHEAD < n:
            queue.append(scores(t + NA_AHEAD))
        finish(t, sm, sc)

    @pl.when(rb == 0)
    def _():
        lo_m = lax.broadcasted_iota(jnp.int32, (N_META, LANES), 1) < DH
        for s in range(N_SLABS):
            sl = slice(s * LANES, (s + 1) * LANES)
            qs = _split_heads(qm_ref[:, sl], lo_m)
            st = _dot_nt(qs, km_ref[:, sl])
            p = jnp.exp2(st - jnp.max(st, axis=-1, keepdims=True))
            o = _dot(p.astype(BF16), vm_ref[:, sl]) / jnp.sum(p, axis=-1, keepdims=True)
            om_ref[:, sl] = jnp.where(lo_m, o[:N_META], o[N_META:]).astype(BF16)


def _na_bias_row(r):
    rs = jnp.clip(r - WIN_R // 2, 0, ROWS - WIN_R)
    return rs - r + (WIN_R - 1)


def _na_attn(qkv_x, qkv_m, bias, mbias):
    nrb = ROWS // NA_RB
    once = pl.Buffered(1)
    bias_specs = [
        pl.BlockSpec((HEADS, None, GRID_W, BAND), lambda b, r, rr=rr: (0, _na_bias_row(NA_RB * r + rr), 0, 0))
        for rr in range(NA_RB)]
    return pl.pallas_call(
        _na_body,
        grid=(B, nrb),
        in_specs=[
            pl.BlockSpec((NA_RB * GRID_W, D), lambda b, r: (b * nrb + r, 0)),
            pl.BlockSpec((SEQ, D), lambda b, r: (b, 1), pipeline_mode=once),
            pl.BlockSpec((SEQ, D), lambda b, r: (b, 2), pipeline_mode=once),
            pl.BlockSpec((N_META, D), lambda b, r: (b, 0)),
            pl.BlockSpec((N_META, D), lambda b, r: (b, 1)),
            pl.BlockSpec((N_META, D), lambda b, r: (b, 2)),
            *bias_specs,
            pl.BlockSpec((N_SLABS, 2 * GRID_W, N_META), lambda b, r: (0, 0, 0)),
        ],
        out_specs=[
            pl.BlockSpec((NA_RB * GRID_W, D), lambda b, r: (b * nrb + r, 0)),
            pl.BlockSpec((N_META, D), lambda b, r: (b, 0)),
        ],
        out_shape=[
            jax.ShapeDtypeStruct((M_X, D), BF16),
            jax.ShapeDtypeStruct((M_META, D), BF16),
        ],
        compiler_params=_params(("parallel", "arbitrary"), 56),
        name="na_attn",
    )(qkv_x, qkv_x, qkv_x, qkv_m, qkv_m, qkv_m, *([bias] * NA_RB), mbias)


def _na_bias_table(rpb):
    cols = np.arange(GRID_W)
    col_start = np.clip(cols - WIN_C // 2, 0, GRID_W - WIN_C)
    kc = np.arange(GRID_W)
    valid = (kc[None, :] >= col_start[:, None]) & (kc[None, :] < col_start[:, None] + WIN_C)
    col_rel = kc[None, :] - cols[:, None] + (WIN_C - 1)
    onehot = np.zeros((GRID_W, GRID_W, 2 * WIN_C - 1), np.float32)
    ci, ki = np.nonzero(valid)
    onehot[ci, ki, col_rel[ci, ki]] = 1.0
    base = jnp.einsum("hrx,ckx->hrck", rpb.astype(F32), onehot, precision=lax.Precision.HIGHEST)
    base = jnp.where(valid[None, None], base, NEG)
    t = jnp.stack([base[:, d:d + WIN_R] for d in range(WIN_R)], axis=1)
    return jnp.transpose(t, (0, 1, 3, 2, 4)).reshape(HEADS, WIN_R, GRID_W, BAND)


def _na_meta_bias_table(meta_bias):
    mb = meta_bias.astype(F32).reshape(N_SLABS, 2, 1, N_META)
    return jnp.broadcast_to(mb, (N_SLABS, 2, GRID_W, N_META)).reshape(N_SLABS, 2 * GRID_W, N_META)


_Q_HEAD_ORDER = np.array([8 * j + 4 * hi + i for j in range(2) for i in range(4) for hi in range(2)])


def _head_norm_rope(x, gain, cos, sin_a, sin_b, lo):
    sq = x * x
    s_lo = jnp.sum(jnp.where(lo, sq, 0.0), axis=-1, keepdims=True)
    s_hi = jnp.sum(jnp.where(lo, 0.0, sq), axis=-1, keepdims=True)
    inv = lax.rsqrt(jnp.where(lo, s_lo, s_hi) * (1.0 / DH) + EPS)
    y = x * inv * gain
    quarter = DH // 4
    return y * cos + pltpu.roll(y, LANES - quarter, 1) * sin_a + pltpu.roll(y, quarter, 1) * sin_b


def _gqa_proj_body(h_ref, g_ref, wq_ref, wk_ref, wv_ref, wvt_ref, qg_ref, kg_ref, cos_ref, sa_ref, sb_ref,
                   q_ref, k_ref, v_ref, vt_ref):
    tm = h_ref.shape[0]
    tkc = vt_ref.shape[2]
    xn = _rms(h_ref[...], g_ref[...]).astype(BF16)
    qf = _dot(xn, wq_ref[...])
    kf = _dot(xn, wk_ref[...])
    v_ref[...] = _dot(xn, wv_ref[...]).astype(BF16)
    vt = _dot_nt(wvt_ref[...], xn).astype(BF16)
    for c in range(tm // tkc):
        vt_ref[c] = vt[:, c * tkc:(c + 1) * tkc]
    lo = lax.broadcasted_iota(jnp.int32, (tm, LANES), 1) < DH
    cos, sa, sb = cos_ref[...], sa_ref[...], sb_ref[...]
    for s in range(N_SLABS):
        sl = slice(s * LANES, (s + 1) * LANES)
        q = _head_norm_rope(qf[:, sl], qg_ref[...], cos, sa, sb, lo)
        q_ref[:, sl] = (q * (DH ** -0.5 * LOG2E)).astype(BF16)
    for s in range(NKV // LANES):
        sl = slice(s * LANES, (s + 1) * LANES)
        k_ref[:, sl] = _head_norm_rope(kf[:, sl], kg_ref[...], cos, sa, sb, lo).astype(BF16)


def _gqa_proj(h, g, wq, wk, wv, wvt, qg, kg, tables, tm):
    m = h.shape[0]
    tkc = min(TK, tm)
    n_tab = tables[0].shape[0] // tm
    tab = lambda i: (i % n_tab, 0)
    const = lambda i: (0, 0)
    return pl.pallas_call(
        _gqa_proj_body,
        grid=(m // tm,),
        in_specs=[
            pl.BlockSpec((tm, D), lambda i: (i, 0)),
            pl.BlockSpec((1, D), const),
            pl.BlockSpec((D, D), const),
            pl.BlockSpec((D, NKV), const),
            pl.BlockSpec((D, NKV), const),
            pl.BlockSpec((NKV, D), const),
            pl.BlockSpec((1, LANES), const),
            pl.BlockSpec((1, LANES), const),
            pl.BlockSpec((tm, LANES), tab),
            pl.BlockSpec((tm, LANES), tab),
            pl.BlockSpec((tm, LANES), tab),
        ],
        out_specs=[
            pl.BlockSpec((tm, D), lambda i: (i, 0)),
            pl.BlockSpec((tm, NKV), lambda i: (i, 0)),
            pl.BlockSpec((tm, NKV), lambda i: (i, 0)),
            pl.BlockSpec((tm // tkc, NKV, tkc), lambda i: (i, 0, 0)),
        ],
        out_shape=[
            jax.ShapeDtypeStruct((m, D), BF16),
            jax.ShapeDtypeStruct((m, NKV), BF16),
            jax.ShapeDtypeStruct((m, NKV), BF16),
            jax.ShapeDtypeStruct((m // tkc, NKV, tkc), BF16),
        ],
        compiler_params=_params(("parallel",), 48),
        name="gqa_proj",
    )(h, g, wq, wk, wv, wvt, qg, kg, *tables)


def _rope_tables():
    t = np.arange(SEQ)
    row = (t // GRID_W).astype(np.float32)
    col = (t % GRID_W).astype(np.float32)
    sec = DH // 2
    freqs = ROPE_THETA ** (-jnp.arange(0, sec, 2, dtype=F32) / sec)
    ang = jnp.stack([row[:, None] * freqs, col[:, None] * freqs], axis=1)
    cos = jnp.cos(ang)
    sin = jnp.sin(ang)
    zero = jnp.zeros_like(sin)
    cos_h = jnp.stack([cos, cos], axis=2).reshape(SEQ, DH)
    sa_h = jnp.stack([-sin, zero], axis=2).reshape(SEQ, DH)
    sb_h = jnp.stack([zero, sin], axis=2).reshape(SEQ, DH)
    reps = LANES // DH
    return tuple(jnp.tile(a, (1, reps)) for a in (cos_h, sa_h, sb_h))


def _identity_rope_tables(m):
    return (jnp.ones((m, LANES), F32), jnp.zeros((m, LANES), F32), jnp.zeros((m, LANES), F32))


def _gqa_body(q_ref, k_ref, vt_ref, vtm_ref, o_ref, qt_ref, m_ref, acc_ref, l_ref, *, shifted):
    row_lo = lax.broadcasted_iota(jnp.int32, (LANES, TQ), 0) < DH

    def head_slices(hd):
        s, half = divmod(hd, 2)
        j = s // 4
        g = 2 * j + half
        return slice(j * LANES, (j + 1) * LANES), slice(g * DH, (g + 1) * DH)

    for s in range(N_SLABS):
        qt = q_ref[:, s * LANES:(s + 1) * LANES].astype(F32).T
        zero = jnp.zeros_like(qt)
        qt_ref[2 * s] = jnp.where(row_lo, qt, zero).astype(BF16)
        qt_ref[2 * s + 1] = jnp.where(row_lo, zero, qt).astype(BF16)

    def chunk(c, first, last):
        rows = TK + N_META if last else TK
        off = c * TK if isinstance(c, int) else pl.multiple_of(c * TK, TK)

        def scores(hd):
            ksl, _ = head_slices(hd)
            return _dot(k_ref[pl.ds(off, rows), ksl], qt_ref[hd])

        pending = scores(0)
        for hd in range(HEADS):
            st = pending
            if hd + 1 < HEADS:
                pending = scores(hd + 1)
            _, vsl = head_slices(hd)
            if shifted:
                m_new = jnp.max(st, axis=0, keepdims=True)
                if not first:
                    m_old = m_ref[hd]
                    m_new = jnp.maximum(m_old, m_new)
                m_ref[hd] = m_new
                st = st - m_new
            pf = jnp.exp2(st)
            lsum = jnp.sum(pf, axis=0, keepdims=True)
            p = pf.astype(BF16)
            pv = _dot(vt_ref[c, vsl, :], p[:TK])
            if last:
                pv = pv + _dot(vtm_ref[vsl, :], p[TK:])
            if first:
                acc_ref[hd] = pv
                l_ref[hd] = lsum
            elif shifted:
                alpha = jnp.exp2(m_old - m_new)
                acc_ref[hd] = alpha * acc_ref[hd] + pv
                l_ref[hd] = alpha * l_ref[hd] + lsum
            else:
                acc_ref[hd] += pv
                l_ref[hd] += lsum

    def middle(c, carry):
        chunk(c, False, False)
        return carry

    chunk(0, True, False)
    lax.fori_loop(1, NKC - 1, middle, 0)
    chunk(NKC - 1, False, True)

    for s in range(N_SLABS):
        ot = jnp.concatenate([acc_ref[2 * s] / l_ref[2 * s], acc_ref[2 * s + 1] / l_ref[2 * s + 1]], axis=0)
        o_ref[:, s * LANES:(s + 1) * LANES] = ot.T.astype(BF16)


def _gqa_attn(q, k_all, vt, vt_meta, shifted):
    return pl.pallas_call(
        functools.partial(_gqa_body, shifted=shifted),
        grid=(B, SEQ // TQ),
        in_specs=[
            pl.BlockSpec((TQ, D), lambda b, i: (b * (SEQ // TQ) + i, 0)),
            pl.BlockSpec((None, SEQ + N_META, NKV), lambda b, i: (b, 0, 0)),
            pl.BlockSpec((NKC, NKV, TK), lambda b, i: (b, 0, 0)),
            pl.BlockSpec((None, NKV, N_META), lambda b, i: (b, 0, 0)),
        ],
        out_specs=pl.BlockSpec((TQ, D), lambda b, i: (b * (SEQ // TQ) + i, 0)),
        out_shape=jax.ShapeDtypeStruct((M_X, D), BF16),
        scratch_shapes=[
            pltpu.VMEM((HEADS, LANES, TQ), BF16),
            pltpu.VMEM((HEADS, 1, TQ), F32),
            pltpu.VMEM((HEADS, DH, TQ), F32),
            pltpu.VMEM((HEADS, 1, TQ), F32),
        ],
        compiler_params=_params(("parallel", "arbitrary"), 48),
        name="gqa_attn",
    )(q, k_all, vt, vt_meta)


def _gqa_meta_body(qm_ref, k_ref, km_ref, v_ref, vm_ref, om_ref):
    lo_m = lax.broadcasted_iota(jnp.int32, (N_META, LANES), 1) < DH
    slabs_per_pair = N_SLABS // (NKV // LANES)
    for j in range(NKV // LANES):
        ksl = slice(j * LANES, (j + 1) * LANES)
        qs = jnp.concatenate(
            [_split_heads(qm_ref[:, s * LANES:(s + 1) * LANES], lo_m)
             for s in range(j * slabs_per_pair, (j + 1) * slabs_per_pair)], axis=0)
        s_m = _dot_nt(qs, km_ref[:, ksl])
        s_x = _dot_nt(qs, k_ref[:, ksl])
        m = jnp.maximum(jnp.max(s_m, axis=-1, keepdims=True), jnp.max(s_x, axis=-1, keepdims=True))
        p_m = jnp.exp2(s_m - m)
        p_x = jnp.exp2(s_x - m)
        l = jnp.sum(p_m, axis=-1, keepdims=True) + jnp.sum(p_x, axis=-1, keepdims=True)
        o = (_dot(p_m.astype(BF16), vm_ref[:, ksl]) + _dot(p_x.astype(BF16), v_ref[:, ksl])) / l
        for i in range(slabs_per_pair):
            s = j * slabs_per_pair + i
            r0 = 2 * N_META * i
            om_ref[:, s * LANES:(s + 1) * LANES] = jnp.where(
                lo_m, o[r0:r0 + N_META], o[r0 + N_META:r0 + 2 * N_META]).astype(BF16)


def _gqa_meta_attn(q_m, k_x, k_m, v_x, v_m):
    return pl.pallas_call(
        _gqa_meta_body,
        grid=(B,),
        in_specs=[
            pl.BlockSpec((N_META, D), lambda b: (b, 0)),
            pl.BlockSpec((SEQ, NKV), lambda b: (b, 0)),
            pl.BlockSpec((N_META, NKV), lambda b: (b, 0)),
            pl.BlockSpec((SEQ, NKV), lambda b: (b, 0)),
            pl.BlockSpec((N_META, NKV), lambda b: (b, 0)),
        ],
        out_specs=pl.BlockSpec((N_META, D), lambda b: (b, 0)),
        out_shape=jax.ShapeDtypeStruct((M_META, D), BF16),
        compiler_params=_params(("parallel",), 48),
        name="gqa_meta_attn",
    )(q_m, k_x, k_m, v_x, v_m)


@jax.jit
def kernel(x, meta_tokens, norm_gains, ffn_w_gate, ffn_w_up, ffn_w_down, na_w_qkv, na_w_o, na_rpb, na_meta_bias,
           gqa_w_qkv, gqa_w_o, gqa_q_gain, gqa_k_gain):
    hx = x.astype(F32).reshape(M_X, D)
    hm = jnp.broadcast_to(meta_tokens[None].astype(F32), (B, N_META, D)).reshape(M_META, D)
    streams = [(hx, TM), (hm, M_META)]
    gains = norm_gains.astype(F32).reshape(DEPTH, 6, 1, D)
    wg = ffn_w_gate.astype(BF16)
    wu = ffn_w_up.astype(BF16)
    wd = ffn_w_down.astype(BF16)
    nq = HEADS * DH

    for i in range(DEPTH):
        g = gains[i]
        j = i // 2
        streams = [(_ffn(h, g[0], g[1], wg[i, 0], wu[i, 0], wd[i, 0], tm), tm) for h, tm in streams]
        if i % 2 == 0:
            w = na_w_qkv[j].astype(BF16)
            qkv_x, qkv_m = [_na_proj(h, g[2], w, tm) for h, tm in streams]
            outs = _na_attn(qkv_x, qkv_m, _na_bias_table(na_rpb[j].astype(F32) * LOG2E),
                            _na_meta_bias_table(na_meta_bias[j].astype(F32) * LOG2E))
            w_o = na_w_o[j].astype(BF16)
        else:
            w = gqa_w_qkv[j]
            wq = w[:, :nq].reshape(D, HEADS, DH)[:, _Q_HEAD_ORDER].reshape(D, nq).astype(BF16)
            wk = w[:, nq:nq + NKV].astype(BF16)
            wv = w[:, nq + NKV:].astype(BF16)
            qg = jnp.tile(gqa_q_gain[j].astype(F32), LANES // DH).reshape(1, LANES)
            kg = jnp.tile(gqa_k_gain[j].astype(F32), LANES // DH).reshape(1, LANES)
            tables = [_rope_tables(), _identity_rope_tables(M_META)]
            (q_x, k_x, v_x, vt_x), (q_m, k_m, v_m, vt_m) = [
                _gqa_proj(h, g[2], wq, wk, wv, wv.T, qg, kg, tab, tm) for (h, tm), tab in zip(streams, tables)]
            k_all = jnp.concatenate([k_x.reshape(B, SEQ, NKV), k_m.reshape(B, N_META, NKV)], axis=1)
            vt_meta = vt_m[0].reshape(NKV, B, N_META).transpose(1, 0, 2)
            bound = DH ** 0.5 * LOG2E * jnp.max(jnp.abs(gqa_q_gain[j])) * jnp.max(jnp.abs(gqa_k_gain[j]))
            o_x = lax.cond(bound <= MAX_DIRECT_EXPONENT,
                           functools.partial(_gqa_attn, shifted=False),
                           functools.partial(_gqa_attn, shifted=True),
                           q_x, k_all, vt_x, vt_meta)
            outs = (o_x, _gqa_meta_attn(q_m, k_x, k_m, v_x, v_m))
            w_o = gqa_w_o[j].reshape(HEADS, DH, D)[_Q_HEAD_ORDER].reshape(nq, D).astype(BF16)
        streams = [(_oproj(o, w_o, h, g[3], tm), tm) for o, (h, tm) in zip(outs, streams)]
        if i == DEPTH - 1:
            streams = streams[:1]
        streams = [(_ffn(h, g[4], g[5], wg[i, 1], wu[i, 1], wd[i, 1], tm), tm) for h, tm in streams]
    return streams[0][0].reshape(B, SEQ, D)
```
